```python
import jax
import jax.numpy as jnp
from jax import lax
import numpy as np

D_MODEL = 4096
BATCH = 4
SEQ = 4096
DEPTH = 4

GRID_W = 64
CTX_LEN = 256
N_MIXERS = 2
N_A_LAYERS = (DEPTH + 1) // 2
N_B_LAYERS = DEPTH // 2
N_MOD = 6
ADA_RANK = 256
MLP_HIDDEN = 4 * D_MODEL
NORM_EPS = 1e-6

HG_HEAD_DIM = 128
HG_HEADS = D_MODEL // HG_HEAD_DIM
HG_CHUNK = 64
HG_N_PROJ = 5

MLA_HEADS = D_MODEL // 128
MLA_NOPE = 128
MLA_ROPE = 64
MLA_V = 128
MLA_Q_RANK = 1024
MLA_KV_RANK = 512
MLA_Q_BLOCK = 128
MLA_SCALE = (MLA_NOPE + MLA_ROPE) ** -0.5
ROPE_THETA = 10000.0
ROPE_AXIS_DIM = MLA_ROPE // 2

kernel_name = 'hybrid_hgrn2_mla_flow_block'


def rms_norm(x, gain):
    xf = x.astype(jnp.float32)
    xf = xf * lax.rsqrt(jnp.mean(xf * xf, axis=-1, keepdims=True) + NORM_EPS)
    return (xf * gain.astype(jnp.float32)).astype(x.dtype)


def ada_modulation(cond, w_down, w_up, bias):
    m = (jax.nn.silu(cond) @ w_down) @ w_up + bias
    return jnp.split(m, N_MOD, axis=-1)


def modulate(h, shift, scale):
    return h * (1 + scale) + shift


def axial_rope_tables(rows, dtype):
    t = jnp.arange(rows * GRID_W)
    row = (t // GRID_W).astype(jnp.float32)
    col = (t % GRID_W).astype(jnp.float32)
    inv_freq = ROPE_THETA ** (-jnp.arange(0, ROPE_AXIS_DIM, 2, dtype=jnp.float32) / ROPE_AXIS_DIM)
    ang_r = row[:, None] * inv_freq
    ang_c = col[:, None] * inv_freq
    ang = jnp.concatenate([ang_r, ang_r, ang_c, ang_c], axis=-1)
    return jnp.cos(ang).astype(dtype), jnp.sin(ang).astype(dtype)


def apply_rope(x, cos, sin):
    r1, r2, c1, c2 = jnp.split(x, 4, axis=-1)
    return x * cos + jnp.concatenate([-r2, r1, -c2, c1], axis=-1) * sin


def mla_queries(cq, q_norm, w_uq):
    B, T, _ = cq.shape
    q = (rms_norm(cq, q_norm) @ w_uq).reshape(B, T, MLA_HEADS, MLA_NOPE + MLA_ROPE)
    return q[..., :MLA_NOPE], q[..., MLA_NOPE:]


def mla_keys_values(ckv, kv_norm, w_ukv):
    B, T, _ = ckv.shape
    kv = (rms_norm(ckv, kv_norm) @ w_ukv).reshape(B, T, MLA_HEADS, MLA_NOPE + MLA_V)
    return kv[..., :MLA_NOPE], kv[..., MLA_NOPE:]


def mla_attend(q_nope, q_rope, k_nope, k_rope, v):
    s = jnp.einsum('bqhd,bkhd->bhqk', q_nope, k_nope) + jnp.einsum('bqhr,bkr->bhqk', q_rope, k_rope)
    p = jax.nn.softmax(s.astype(jnp.float32) * MLA_SCALE, axis=-1).astype(v.dtype)
    return jnp.einsum('bhqk,bkhd->bqhd', p, v)


def mla_mixer(h_lat, h_ctx, cos, sin, w_down, q_norm, w_uq, kv_norm, w_ukv, w_out, with_ctx_out):
    B, T, _ = h_lat.shape
    L = h_ctx.shape[1]
    split_at = [MLA_Q_RANK, MLA_Q_RANK + MLA_KV_RANK]
    cq_l, ckv_l, kr_l = jnp.split(h_lat @ w_down, split_at, axis=-1)
    cq_c, ckv_c, kr_c = jnp.split(h_ctx @ w_down, split_at, axis=-1)
    qn_l, qr_l = mla_queries(cq_l, q_norm, w_uq)
    qr_l = apply_rope(qr_l, cos[:, None, :], sin[:, None, :])
    kr_l = apply_rope(kr_l, cos, sin)
    kn_l, v_l = mla_keys_values(ckv_l, kv_norm, w_ukv)
    kn_c, v_c = mla_keys_values(ckv_c, kv_norm, w_ukv)
    kn = jnp.concatenate([kn_c, kn_l], axis=1)
    kr = jnp.concatenate([kr_c, kr_l], axis=1)
    v = jnp.concatenate([v_c, v_l], axis=1)
    nb = T // MLA_Q_BLOCK

    def to_blocks(a):
        return a.reshape(B, nb, MLA_Q_BLOCK, *a.shape[2:]).swapaxes(0, 1)

    o = lax.map(lambda qb: mla_attend(qb[0], qb[1], kn, kr, v), (to_blocks(qn_l), to_blocks(qr_l)))
    y_lat = o.swapaxes(0, 1).reshape(B, T, MLA_HEADS * MLA_V) @ w_out
    if not with_ctx_out:
        return y_lat, None
    qn_c, qr_c = mla_queries(cq_c, q_norm, w_uq)
    y_ctx = mla_attend(qn_c, qr_c, kn_c, kr_c, v_c).reshape(B, L, MLA_HEADS * MLA_V) @ w_out
    return y_lat, y_ctx


def gla_scan(q, k, v, log_f, s0):
    B, T, H, K = q.shape
    n = T // HG_CHUNK

    def to_chunks(a):
        return a.reshape(B, n, HG_CHUNK, H, a.shape[-1]).transpose(1, 0, 3, 2, 4)

    incl = jnp.tril(jnp.ones((HG_CHUNK, HG_CHUNK), dtype=bool))[None, None, :, :, None]

    def step(S, xs):
        qc, kc, vc, gc = xs
        b = jnp.cumsum(gc, axis=2)
        o_inter = jnp.einsum('bhtk,bhkv->bhtv', qc * jnp.exp(b), S)
        rel = jnp.where(incl, b[:, :, :, None, :] - b[:, :, None, :, :], -jnp.inf)
        att = jnp.einsum('bhtk,bhsk,bhtsk->bhts', qc, kc, jnp.exp(rel))
        o = o_inter + jnp.einsum('bhts,bhsv->bhtv', att, vc)
        b_last = b[:, :, -1:, :]
        S_new = (jnp.exp(b_last[:, :, 0, :])[..., None] * S
                 + jnp.einsum('bhsk,bhsv->bhkv', kc * jnp.exp(b_last - b), vc))
        return S_new, o

    S_final, o = lax.scan(step, s0, (to_chunks(q), to_chunks(k), to_chunks(v), to_chunks(log_f)))
    o = o.transpose(1, 0, 3, 2, 4).reshape(B, T, H, v.shape[-1])
    return o, S_final


def hgrn2_mixer(h_lat, h_ctx, w_in, lower_bound, g_norm, w_out, with_ctx_out):
    lb = lower_bound.astype(jnp.float32)
    log_lb, log_1m_lb = jnp.log(lb), jnp.log1p(-lb)

    def project(h):
        B, T, _ = h.shape
        heads = lambda a: a.astype(jnp.float32).reshape(B, T, HG_HEADS, HG_HEAD_DIM)
        q, i, zf_fwd, zf_bwd, g = jnp.split(h @ w_in, HG_N_PROJ, axis=-1)

        def log_forget(z, d):
            return heads(jnp.logaddexp(log_lb[d], log_1m_lb[d] + jax.nn.log_sigmoid(z.astype(jnp.float32))))

        return heads(jax.nn.silu(q)), heads(i), log_forget(zf_fwd, 0), log_forget(zf_bwd, 1), g

    def run(h, s0_fwd, s0_bwd):
        q, i, lf_f, lf_b, g = project(h)
        flip = lambda a: jnp.flip(a, axis=1)
        o_f, s_f = gla_scan(q, -jnp.expm1(lf_f), i, lf_f, s0_fwd)
        o_b, s_b = gla_scan(flip(q), -jnp.expm1(flip(lf_b)), flip(i), flip(lf_b), s0_bwd)
        return o_f + flip(o_b), g, s_f, s_b

    def readout(o, g):
        B, T = o.shape[:2]
        o = o.reshape(B, T, D_MODEL).astype(g.dtype)
        return (rms_norm(o, g_norm) * jax.nn.silu(g)) @ w_out

    B = h_ctx.shape[0]
    s0 = jnp.zeros((B, HG_HEADS, HG_HEAD_DIM, HG_HEAD_DIM), jnp.float32)
    o_c, g_c, s_f, s_b = run(h_ctx, s0, s0)
    o_l, g_l, _, _ = run(h_lat, s_f, s_b)
    y_lat = readout(o_l, g_l)
    return y_lat, (readout(o_c, g_c) if with_ctx_out else None)


def sq_relu_mlp(h, w1, w2):
    return jnp.square(jax.nn.relu(h @ w1)) @ w2


def setup_inputs(seed: int = 0) -> dict:
    key = jax.random.key(seed)
    ks = jax.random.split(key, 23)
    f32 = jnp.float32

    def dense(k, shape, fan_in, gain=1.0):
        return jax.random.normal(k, shape, f32) * (gain * fan_in ** -0.5)

    def norm_gain(k, shape):
        return 1.0 + 0.05 * jax.random.normal(k, shape, f32)

    return {
        'x': jax.random.normal(ks[0], (BATCH, SEQ, D_MODEL), f32),
        'c': jax.random.normal(ks[1], (BATCH, D_MODEL), f32),
        'ctx': jax.random.normal(ks[2], (BATCH, CTX_LEN, D_MODEL), f32),
        'c_ctx': jax.random.normal(ks[3], (D_MODEL,), f32),
        'ada_down': dense(ks[4], (DEPTH, D_MODEL, ADA_RANK), D_MODEL),
        'ada_up': dense(ks[5], (DEPTH, ADA_RANK, N_MOD * D_MODEL), ADA_RANK, 0.5),
        'ada_bias': 0.02 * jax.random.normal(ks[6], (DEPTH, N_MOD * D_MODEL), f32),
        'norm_mix_pre': norm_gain(ks[7], (DEPTH, D_MODEL)),
        'norm_mix_post': norm_gain(ks[8], (DEPTH, D_MODEL)),
        'norm_mlp_pre': norm_gain(ks[9], (DEPTH, D_MODEL)),
        'norm_mlp_post': norm_gain(ks[10], (DEPTH, D_MODEL)),
        'mlp_w1': dense(ks[11], (DEPTH, D_MODEL, MLP_HIDDEN), D_MODEL),
        'mlp_w2': dense(ks[12], (DEPTH, MLP_HIDDEN, D_MODEL), MLP_HIDDEN),
        'hg_w_in': dense(ks[13], (N_A_LAYERS, D_MODEL, HG_N_PROJ * D_MODEL), D_MODEL),
        'hg_lb_logits': 0.5 * jax.random.normal(ks[14], (N_A_LAYERS, 2, D_MODEL), f32),
        'hg_norm': norm_gain(ks[15], (N_A_LAYERS, D_MODEL)),
        'hg_w_out': dense(ks[16], (N_A_LAYERS, D_MODEL, D_MODEL), D_MODEL),
        'mla_w_down': dense(ks[17], (N_B_LAYERS, D_MODEL, MLA_Q_RANK + MLA_KV_RANK + MLA_ROPE), D_MODEL),
        'mla_q_norm': norm_gain(ks[18], (N_B_LAYERS, MLA_Q_RANK)),
        'mla_w_uq': dense(ks[19], (N_B_LAYERS, MLA_Q_RANK, MLA_HEADS * (MLA_NOPE + MLA_ROPE)), MLA_Q_RANK),
        'mla_kv_norm': norm_gain(ks[20], (N_B_LAYERS, MLA_KV_RANK)),
        'mla_w_ukv': dense(ks[21], (N_B_LAYERS, MLA_KV_RANK, MLA_HEADS * (MLA_NOPE + MLA_V)), MLA_KV_RANK),
        'mla_w_out': dense(ks[22], (N_B_LAYERS, MLA_HEADS * MLA_V, D_MODEL), MLA_HEADS * MLA_V),
    }


def reference(x, c, ctx, c_ctx, ada_down, ada_up, ada_bias, norm_mix_pre, norm_mix_post, norm_mlp_pre,
              norm_mlp_post, mlp_w1, mlp_w2, hg_w_in, hg_lb_logits, hg_norm, hg_w_out, mla_w_down,
              mla_q_norm, mla_w_uq, mla_kv_norm, mla_w_ukv, mla_w_out):
    rows = x.shape[1] // GRID_W
    cos, sin = axial_rope_tables(rows, x.dtype)
    lb = jnp.cumsum(jax.nn.softmax(hg_lb_logits.astype(jnp.float32), axis=0), axis=0)
    lb = lb - lb[:1]
    x_lat, x_ctx = x, ctx
    for layer in range(DEPTH):
        last = layer == DEPTH - 1
        j = layer // N_MIXERS
        sh1, sc1, gt1, sh2, sc2, gt2 = ada_modulation(c[:, None, :], ada_down[layer], ada_up[layer], ada_bias[layer])
        csh1, csc1, cgt1, csh2, csc2, cgt2 = ada_modulation(c_ctx[None, None, :], ada_down[layer], ada_up[layer], ada_bias[layer])
        h_lat = modulate(rms_norm(x_lat, norm_mix_pre[layer]), sh1, sc1)
        h_ctx = modulate(rms_norm(x_ctx, norm_mix_pre[layer]), csh1, csc1)
        if layer % N_MIXERS == 0:
            y_lat, y_ctx = hgrn2_mixer(h_lat, h_ctx, hg_w_in[j], lb[j], hg_norm[j], hg_w_out[j], not last)
        else:
            y_lat, y_ctx = mla_mixer(h_lat, h_ctx, cos, sin, mla_w_down[j], mla_q_norm[j], mla_w_uq[j],
                                     mla_kv_norm[j], mla_w_ukv[j], mla_w_out[j], not last)
        x_lat = x_lat + gt1 * rms_norm(y_lat, norm_mix_post[layer])
        m_lat = sq_relu_mlp(modulate(rms_norm(x_lat, norm_mlp_pre[layer]), sh2, sc2), mlp_w1[layer], mlp_w2[layer])
        x_lat = x_lat + gt2 * rms_norm(m_lat, norm_mlp_post[layer])
        if not last:
            x_ctx = x_ctx + cgt1 * rms_norm(y_ctx, norm_mix_post[layer])
            m_ctx = sq_relu_mlp(modulate(rms_norm(x_ctx, norm_mlp_pre[layer]), csh2, csc2), mlp_w1[layer], mlp_w2[layer])
            x_ctx = x_ctx + cgt2 * rms_norm(m_ctx, norm_mlp_post[layer])
    return x_lat
```

```python
import functools
import math

import numpy as np
import jax
import jax.numpy as jnp
from jax import lax
from jax.experimental import pallas as pl
from jax.experimental.pallas import tpu as pltpu

F32 = jnp.float32
BF16 = jnp.bfloat16

NORM_EPS = 1e-6
N_MOD = 6
GRID_W = 64
HEAD_DIM = 128
MLA_ROPE = 64
MLA_Q_RANK = 1024
MLA_KV_RANK = 512
MLA_SCALE = (HEAD_DIM + MLA_ROPE) ** -0.5
ROPE_THETA = 10000.0
GLA_CHUNK = 64
GLA_LEVELS = 6

V7X_LANES = 128
V7X_VMEM_LIMIT = 48 * 1024 * 1024

_NT = (((1,), (1,)), ((), ()))


def _tile(dim, pref, align=V7X_LANES):
    if dim <= pref:
        return dim
    t = (pref // align) * align
    while t >= align:
        if dim % t == 0:
            return t
        t -= align
    return dim


def _params(sem):
    return pltpu.CompilerParams(dimension_semantics=sem, vmem_limit_bytes=V7X_VMEM_LIMIT)


def _sigmoid(x):
    return 1.0 / (1.0 + jnp.exp(-x))


def _split3(x):
    hi = x.astype(BF16)
    r = x - hi.astype(F32)
    mid = r.astype(BF16)
    lo = (r - mid.astype(F32)).astype(BF16)
    return hi, mid, lo


def _mm_body(a_ref, b_ref, *rest, nk, n_aux, epilogue):
    aux = rest[:n_aux]
    o_ref = rest[n_aux]
    if nk == 1:
        epilogue(jnp.dot(a_ref[...], b_ref[...], preferred_element_type=F32), o_ref, *aux)
        return
    acc_ref = rest[n_aux + 1]
    k = pl.program_id(2)

    @pl.when(k == 0)
    def _():
        acc_ref[...] = jnp.dot(a_ref[...], b_ref[...], preferred_element_type=F32)

    @pl.when(k > 0)
    def _():
        acc_ref[...] += jnp.dot(a_ref[...], b_ref[...], preferred_element_type=F32)

    @pl.when(k == nk - 1)
    def _():
        epilogue(acc_ref[...], o_ref, *aux)


def _epi_plain(acc, o_ref):
    o_ref[...] = acc.astype(o_ref.dtype)


def _epi_relu2(acc, o_ref):
    r = jnp.maximum(acc, 0.0)
    o_ref[...] = (r * r).astype(o_ref.dtype)


def matmul(a, b, *, out_dtype, epilogue=_epi_plain, aux=(), out_shape=None, out_spec=None,
           tm=1024, tn=1024, tk=512, name="mm"):
    M, K = a.shape
    _, N = b.shape
    tm, tn, tk = _tile(M, tm, 8), _tile(N, tn), _tile(K, tk)
    nk = K // tk
    if out_shape is None:
        out_shape = jax.ShapeDtypeStruct((M, N), out_dtype)
        out_spec = pl.BlockSpec((tm, tn), lambda i, j, k: (i, j))
    in_specs = [pl.BlockSpec((tm, tk), lambda i, j, k: (i, k)),
                pl.BlockSpec((tk, tn), lambda i, j, k: (k, j))]
    in_specs += [spec for _, spec in aux]
    scratch = [] if nk == 1 else [pltpu.VMEM((tm, tn), F32)]
    return pl.pallas_call(
        functools.partial(_mm_body, nk=nk, n_aux=len(aux), epilogue=epilogue),
        grid=(M // tm, N // tn, nk),
        in_specs=in_specs,
        out_specs=out_spec,
        out_shape=out_shape,
        scratch_shapes=scratch,
        compiler_params=_params(("parallel", "parallel", "arbitrary")),
        name=name,
    )(a, b, *[arr for arr, _ in aux])


def _dot3(a, b):
    a0, a1, a2 = _split3(a)
    b0, b1, b2 = _split3(b)
    d = lambda x, y: jnp.dot(x, y, preferred_element_type=F32)
    return ((d(a0, b0) + (d(a0, b1) + d(a1, b0)))
            + ((d(a0, b2) + d(a2, b0)) + d(a1, b1)))


def _ada_body(cond_ref, down_ref, up_ref, bias_ref, o_ref):
    cond = cond_ref[...]
    s = cond * _sigmoid(cond)
    t = _dot3(s, down_ref[...])
    o_ref[...] = _dot3(t, up_ref[...]) + bias_ref[...]


def ada_modulations(cond, ada_down, ada_up, ada_bias):
    depth, D, rank = ada_down.shape
    R = cond.shape[0]
    N = ada_up.shape[2]
    tn = _tile(N, 2048)
    return pl.pallas_call(
        _ada_body,
        grid=(depth, N // tn),
        in_specs=[pl.BlockSpec((R, D), lambda l, j: (0, 0)),
                  pl.BlockSpec((None, D, rank), lambda l, j: (l, 0, 0)),
                  pl.BlockSpec((None, rank, tn), lambda l, j: (l, 0, j)),
                  pl.BlockSpec((None, 1, tn), lambda l, j: (l, 0, j))],
        out_specs=pl.BlockSpec((None, R, tn), lambda l, j: (l, 0, j)),
        out_shape=jax.ShapeDtypeStruct((depth, R, N), F32),
        compiler_params=_params(("parallel", "arbitrary")),
        name="ada_mod",
    )(cond, ada_down, ada_up, ada_bias.reshape(depth, 1, N))


def _rms(x, gain):
    ms = jnp.mean(x * x, axis=-1, keepdims=True)
    return x * lax.rsqrt(ms + NORM_EPS) * gain


def _pre_body(x_ref, g_ref, sh_ref, sc_ref, h_ref):
    h = _rms(x_ref[...], g_ref[...])
    h_ref[...] = (h * (1.0 + sc_ref[...]) + sh_ref[...]).astype(h_ref.dtype)


def norm_modulate(x, gain, shift, scale, group):
    M, D = x.shape
    tr = _tile(group, 256, 8)
    per = group // tr
    row = pl.BlockSpec((tr, D), lambda i: (i, 0))
    vec = pl.BlockSpec((1, D), lambda i: (0, 0))
    mod = pl.BlockSpec((None, 1, D), lambda i: (i // per, 0, 0))
    return pl.pallas_call(
        _pre_body,
        grid=(M // tr,),
        in_specs=[row, vec, mod, mod],
        out_specs=row,
        out_shape=jax.ShapeDtypeStruct((M, D), BF16),
        compiler_params=_params(("parallel",)),
        name="norm_modulate",
    )(x, gain.reshape(1, D), shift, scale)


def _post_body(x_ref, y_ref, gpost_ref, gate_ref, *rest, with_next):
    x = x_ref[...] + gate_ref[...] * _rms(y_ref[...], gpost_ref[...])
    if with_next:
        gpre_ref, sh_ref, sc_ref, xo_ref, h_ref = rest
        xo_ref[...] = x
        h = _rms(x, gpre_ref[...])
        h_ref[...] = (h * (1.0 + sc_ref[...]) + sh_ref[...]).astype(h_ref.dtype)
    else:
        (xo_ref,) = rest
        xo_ref[...] = x


def residual_norm(x, y, g_post, gate, group, nxt=None):
    M, D = x.shape
    tr = _tile(group, 256, 8)
    per = group // tr
    row = pl.BlockSpec((tr, D), lambda i: (i, 0))
    vec = pl.BlockSpec((1, D), lambda i: (0, 0))
    mod = pl.BlockSpec((None, 1, D), lambda i: (i // per, 0, 0))
    args = [x, y, g_post.reshape(1, D), gate]
    in_specs = [row, row, vec, mod]
    out_shape = [jax.ShapeDtypeStruct((M, D), F32)]
    out_specs = [row]
    if nxt is not None:
        gain, shift, scale = nxt
        args += [gain.reshape(1, D), shift, scale]
        in_specs += [vec, mod, mod]
        out_shape.append(jax.ShapeDtypeStruct((M, D), BF16))
        out_specs.append(row)
    out = pl.pallas_call(
        functools.partial(_post_body, with_next=nxt is not None),
        grid=(M // tr,),
        in_specs=in_specs,
        out_specs=out_specs,
        out_shape=out_shape,
        input_output_aliases={0: 0},
        compiler_params=_params(("parallel",)),
        name="residual_norm",
    )(*args)
    return (out[0], out[1]) if nxt is not None else (out[0], None)


def _readout_body(o_ref, g_ref, gain_ref, h_ref):
    g = g_ref[...]
    h_ref[...] = (_rms(o_ref[...], gain_ref[...]) * (g * _sigmoid(g))).astype(h_ref.dtype)


def hgrn2_readout_input(o, proj, gain):
    M, D = o.shape
    tr = _tile(M, 256, 8)
    row = pl.BlockSpec((tr, D), lambda i: (i, 0))
    return pl.pallas_call(
        _readout_body,
        grid=(M // tr,),
        in_specs=[row, pl.BlockSpec((None, tr, D), lambda i: (4, i, 0)),
                  pl.BlockSpec((1, D), lambda i: (0, 0))],
        out_specs=row,
        out_shape=jax.ShapeDtypeStruct((M, D), BF16),
        compiler_params=_params(("parallel",)),
        name="hgrn2_readout_input",
    )(o, proj, gain.reshape(1, D))


def _latent_norm_body(c_ref, qg_ref, kvg_ref, cq_ref, ckv_ref):
    c = c_ref[...]
    cq_ref[...] = _rms(c[:, :MLA_Q_RANK], qg_ref[...]).astype(cq_ref.dtype)
    ckv_ref[...] = _rms(c[:, MLA_Q_RANK:], kvg_ref[...]).astype(ckv_ref.dtype)


def mla_latent_norms(c, q_gain, kv_gain):
    M, W = c.shape
    tr = _tile(M, 512, 8)
    return pl.pallas_call(
        _latent_norm_body,
        grid=(M // tr,),
        in_specs=[pl.BlockSpec((tr, W), lambda i: (i, 0)),
                  pl.BlockSpec((1, MLA_Q_RANK), lambda i: (0, 0)),
                  pl.BlockSpec((1, MLA_KV_RANK), lambda i: (0, 0))],
        out_specs=[pl.BlockSpec((tr, MLA_Q_RANK), lambda i: (i, 0)),
                   pl.BlockSpec((tr, MLA_KV_RANK), lambda i: (i, 0))],
        out_shape=[jax.ShapeDtypeStruct((M, MLA_Q_RANK), BF16),
                   jax.ShapeDtypeStruct((M, MLA_KV_RANK), BF16)],
        compiler_params=_params(("parallel",)),
        name="mla_latent_norms",
    )(c, q_gain.reshape(1, -1), kv_gain.reshape(1, -1))


def _hg_proj_epilogue(acc, o_ref, llb_ref, l1m_ref, *, tiles_per_section):
    sec = pl.program_id(1) // tiles_per_section

    @pl.when(sec == 0)
    def _():
        o_ref[...] = acc * _sigmoid(acc)

    @pl.when(jnp.logical_or(sec == 1, sec == 4))
    def _():
        o_ref[...] = acc

    @pl.when(jnp.logical_or(sec == 2, sec == 3))
    def _():
        log_sig = jnp.minimum(acc, 0.0) - jnp.log1p(jnp.exp(-jnp.abs(acc)))
        a = llb_ref[...]
        b = l1m_ref[...] + log_sig
        o_ref[...] = jnp.maximum(a, b) + jnp.log1p(jnp.exp(-jnp.abs(a - b)))


def hgrn2_project(h, w_in, log_lb, log_1m_lb):
    M, D = h.shape
    tm, tn = _tile(M, 1024, 8), _tile(D, 1024)
    per = D // tn
    zeros = jnp.zeros((1, D), F32)
    aux_a = jnp.concatenate([zeros, zeros, log_lb, zeros], axis=0).reshape(5, 1, D)
    aux_b = jnp.concatenate([zeros, zeros, log_1m_lb, zeros], axis=0).reshape(5, 1, D)
    aux_spec = pl.BlockSpec((None, 1, tn), lambda i, j, k: (j // per, 0, j % per))
    return matmul(
        h, w_in, out_dtype=F32,
        epilogue=functools.partial(_hg_proj_epilogue, tiles_per_section=per),
        aux=((aux_a, aux_spec), (aux_b, aux_spec)),
        out_shape=jax.ShapeDtypeStruct((5, M, D), F32),
        out_spec=pl.BlockSpec((None, tm, tn), lambda i, j, k: (j // per, i, j % per)),
        tm=tm, tn=tn, name="hgrn2_project")


def _gla_tables(reverse):
    C, L = GLA_CHUNK, GLA_LEVELS
    r = np.arange(C)
    mats = []
    if not reverse:
        mats.append((r[None, :] <= r[:, None]))
        mats.append((r[None, :] > r[:, None]))
    else:
        mats.append((r[None, :] >= r[:, None]))
        mats.append((r[None, :] < r[:, None]))
    lvl = np.full((C, C), -1, np.int32)
    lvl[r, r] = L
    for l in range(L):
        w = C >> (l + 1)
        blk = r // w
        odd = (blk % 2) == 1
        rho = ((blk // 2) * 2 + 1) * w
        m = np.zeros((C, C), bool)
        for t in range(C):
            if not reverse:
                cols = (r >= rho[t]) & (r <= t) if odd[t] else (r > t) & (r < rho[t])
            else:
                cols = (r >= rho[t]) & (r < t) if odd[t] else (r >= t) & (r < rho[t])
            m[t] = cols
        mats.append(m)
        same_parent = (r[:, None] // (2 * w)) == (r[None, :] // (2 * w))
        if not reverse:
            pair = same_parent & odd[:, None] & ~odd[None, :]
        else:
            pair = same_parent & ~odd[:, None] & odd[None, :]
        lvl[pair] = l
    tail = np.zeros((16, C), bool)
    tail[0] = True
    mats.append(tail)
    m = np.concatenate(mats, axis=0).astype(np.float32)
    return np.concatenate([m, m, m], axis=1), lvl


def _gla_body(q_ref, v_ref, g_ref, mcat_ref, lvl_ref, *rest, heads, accumulate):
    if accumulate:
        oprev_ref, o_ref, st_ref, ex_ref = rest
    else:
        o_ref, st_ref, ex_ref = rest
    C, L = GLA_CHUNK, GLA_LEVELS

    @pl.when(pl.program_id(2) == 0)
    def _():
        st_ref[...] = jnp.zeros_like(st_ref)

    g = g_ref[...]
    ex_ref[...] = jnp.dot(mcat_ref[...], jnp.concatenate(_split3(g), axis=0),
                          preferred_element_type=F32)
    lvl = lvl_ref[...]
    for h in range(heads):
        sl = slice(h * HEAD_DIM, (h + 1) * HEAD_DIM)
        q = q_ref[:, sl]
        v = v_ref[:, sl]
        vb = v.astype(BF16)
        k = 1.0 - jnp.exp(g[:, sl])
        st = st_ref[h]
        qe = (q * jnp.exp(ex_ref[0:C, sl])).astype(BF16)
        o = lax.dot_general(qe, st.astype(BF16), _NT, preferred_element_type=F32)
        att = jnp.zeros((C, C), F32)
        for l in range(L):
            e = jnp.exp(ex_ref[(2 + l) * C:(3 + l) * C, sl])
            p = lax.dot_general((q * e).astype(BF16), (k * e).astype(BF16), _NT,
                                preferred_element_type=F32)
            att = jnp.where(lvl == l, p, att)
        p = lax.dot_general(q.astype(BF16), k.astype(BF16), _NT, preferred_element_type=F32)
        att = jnp.where(lvl == L, p, att)
        o = o + jnp.dot(att.astype(BF16), vb, preferred_element_type=F32)
        if accumulate:
            o = o + oprev_ref[:, sl]
        o_ref[:, sl] = o
        ke = (k * jnp.exp(ex_ref[C:2 * C, sl])).astype(BF16)
        decay = jnp.exp(ex_ref[(2 + L) * C:(2 + L) * C + 1, sl])
        st_ref[h] = decay * st + jnp.dot(v.T.astype(BF16), ke, preferred_element_type=F32)


def gla_scan(proj, *, batch, seq, ctx_len, reverse, o_prev=None):
    _, M, D = proj.shape
    C = GLA_CHUNK
    heads = min(8, D // HEAD_DIM)
    width = heads * HEAD_DIM
    n_ctx, n_lat = ctx_len // C, seq // C
    ctx_base = batch * seq // C
    mcat, lvl = _gla_tables(reverse)
    mcat = jnp.asarray(mcat, BF16)
    lvl = jnp.asarray(lvl)
    g_section = 3 if reverse else 2

    def row_block(b, n):
        if reverse:
            return jnp.where(n < n_ctx, ctx_base + b * n_ctx + (n_ctx - 1 - n),
                             b * n_lat + (n_lat - 1 - (n - n_ctx)))
        return jnp.where(n < n_ctx, ctx_base + b * n_ctx + n, b * n_lat + (n - n_ctx))

    def section(s):
        return pl.BlockSpec((None, C, width), lambda b, hg, n: (s, row_block(b, n), hg))

    row = pl.BlockSpec((C, width), lambda b, hg, n: (row_block(b, n), hg))
    const = lambda shape: pl.BlockSpec(shape, lambda b, hg, n: (0, 0))
    args = [proj, proj, proj, mcat, lvl]
    in_specs = [section(0), section(1), section(g_section), const(mcat.shape), const(lvl.shape)]
    aliases = {}
    if o_prev is not None:
        args.append(o_prev)
        in_specs.append(row)
        aliases = {5: 0}
    return pl.pallas_call(
        functools.partial(_gla_body, heads=heads, accumulate=o_prev is not None),
        grid=(batch, D // width, n_ctx + n_lat),
        in_specs=in_specs,
        out_specs=row,
        out_shape=jax.ShapeDtypeStruct((M, D), F32),
        scratch_shapes=[pltpu.VMEM((heads, HEAD_DIM, HEAD_DIM), F32),
                        pltpu.VMEM(mcat.shape[:1] + (width,), F32)],
        input_output_aliases=aliases,
        compiler_params=_params(("parallel", "parallel", "arbitrary")),
        name="gla_bwd" if reverse else "gla_fwd",
    )(*args)


def _rope_epilogue(acc, o_ref, cos_ref, sin_ref):
    cos, sin = cos_ref[...], sin_ref[...]
    lane = lax.broadcasted_iota(jnp.int32, cos.shape, 1)
    first_half = (lane % 32) < 16
    for hh in range(acc.shape[1] // V7X_LANES):
        sl = slice(hh * V7X_LANES, (hh + 1) * V7X_LANES)
        x = acc[:, sl]
        rot = jnp.where(first_half, -pltpu.roll(x, V7X_LANES - 16, 1), pltpu.roll(x, 16, 1))
        o_ref[:, sl] = (x * cos + rot * sin).astype(o_ref.dtype)


def rope_matmul(a, w, cos, sin, name):
    M = a.shape[0]
    tm = _tile(M, 1024, 8)
    tab = pl.BlockSpec((tm, V7X_LANES), lambda i, j, k: (i, 0))
    return matmul(a, w, out_dtype=BF16, epilogue=_rope_epilogue,
                  aux=((cos, tab), (sin, tab)), tm=tm, name=name)


def _attn_body(qn_ref, qr_ref, *rest, with_latent_keys):
    if with_latent_keys:
        knc_ref, krc_ref, vc_ref, knl_ref, krl_ref, vl_ref, o_ref = rest
    else:
        knc_ref, krc_ref, vc_ref, o_ref = rest
    qn, qr = qn_ref[...], qr_ref[...]

    def scores(kn_ref, kr_ref):
        s = (lax.dot_general(qn, kn_ref[...], _NT, preferred_element_type=F32)
             + lax.dot_general(qr, kr_ref[...], _NT, preferred_element_type=F32))
        return s * MLA_SCALE

    s_c = scores(knc_ref, krc_ref)
    m = jnp.max(s_c, axis=-1, keepdims=True)
    if with_latent_keys:
        s_l = scores(knl_ref, krl_ref)
        m = jnp.maximum(m, jnp.max(s_l, axis=-1, keepdims=True))
    p_c = jnp.exp(s_c - m)
    denom = jnp.sum(p_c, axis=-1, keepdims=True)
    o = jnp.dot(p_c.astype(BF16), vc_ref[...], preferred_element_type=F32)
    if with_latent_keys:
        p_l = jnp.exp(s_l - m)
        denom = denom + jnp.sum(p_l, axis=-1, keepdims=True)
        o = o + jnp.dot(p_l.astype(BF16), vl_ref[...], preferred_element_type=F32)
    o_ref[...] = (o / denom).astype(o_ref.dtype)


def mla_attention(qn, qr, kn, kr, v, *, batch, seq, ctx_len):
    M, D = qn.shape
    H = D // HEAD_DIM
    tq = _tile(seq, 256, 8)
    nq = seq // tq
    ctx_base = batch * seq // ctx_len
    blk = lambda rows, fn: pl.BlockSpec((rows, HEAD_DIM), fn)
    ctx_keys = [blk(ctx_len, lambda b, h, i: (ctx_base + b, h)),
                blk(ctx_len, lambda b, h, i: (ctx_base + b, 0)),
                blk(ctx_len, lambda b, h, i: (ctx_base + b, h))]
    lat_keys = [blk(seq, lambda b, h, i: (b, h)),
                blk(seq, lambda b, h, i: (b, 0)),
                blk(seq, lambda b, h, i: (b, h))]
    q_lat = blk(tq, lambda b, h, i: (b * nq + i, h))
    o_lat = pl.pallas_call(
        functools.partial(_attn_body, with_latent_keys=True),
        grid=(batch, H, nq),
        in_specs=[q_lat, q_lat] + ctx_keys + lat_keys,
        out_specs=q_lat,
        out_shape=jax.ShapeDtypeStruct((M, D), BF16),
        compiler_params=_params(("parallel", "parallel", "arbitrary")),
        name="mla_attention_latent",
    )(qn, qr, kn, kr, v, kn, kr, v)
    q_ctx = blk(ctx_len, lambda b, h, i: (ctx_base + b, h))
    return pl.pallas_call(
        lambda qn_ref, qr_ref, kn_ref, kr_ref, v_ref, _, o_ref: _attn_body(
            qn_ref, qr_ref, kn_ref, kr_ref, v_ref, o_ref, with_latent_keys=False),
        grid=(batch, H, 1),
        in_specs=[q_ctx, q_ctx] + ctx_keys + [pl.BlockSpec(memory_space=pl.ANY)],
        out_specs=q_ctx,
        out_shape=jax.ShapeDtypeStruct((M, D), BF16),
        input_output_aliases={5: 0},
        compiler_params=_params(("parallel", "parallel", "arbitrary")),
        name="mla_attention_context",
    )(qn, qr, kn, kr, v, o_lat)


def _rope_tables(batch, seq, ctx_len):
    t = jnp.arange(seq)
    row = (t // GRID_W).astype(F32)
    col = (t % GRID_W).astype(F32)
    axis_dim = MLA_ROPE // 2
    inv_freq = ROPE_THETA ** (-jnp.arange(0, axis_dim, 2, dtype=F32) / axis_dim)
    ang_r = row[:, None] * inv_freq
    ang_c = col[:, None] * inv_freq
    ang = jnp.concatenate([ang_r, ang_r, ang_c, ang_c], axis=-1)
    pad = V7X_LANES - MLA_ROPE
    cos = jnp.pad(jnp.cos(ang), ((0, 0), (0, pad)), constant_values=1.0)
    sin = jnp.pad(jnp.sin(ang), ((0, 0), (0, pad)))
    n_ctx = batch * ctx_len
    cos = jnp.concatenate([jnp.tile(cos, (batch, 1)), jnp.ones((n_ctx, V7X_LANES), F32)], axis=0)
    sin = jnp.concatenate([jnp.tile(sin, (batch, 1)), jnp.zeros((n_ctx, V7X_LANES), F32)], axis=0)
    return cos, sin


def _pad_heads(w, heads, width):
    K = w.shape[0]
    w = w.reshape(K, heads, width)
    w = jnp.pad(w, ((0, 0), (0, 0), (0, V7X_LANES - width)))
    return w.reshape(K, heads * V7X_LANES).astype(BF16)


def kernel(x, c, ctx, c_ctx, ada_down, ada_up, ada_bias, norm_mix_pre, norm_mix_post, norm_mlp_pre,
           norm_mlp_post, mlp_w1, mlp_w2, hg_w_in, hg_lb_logits, hg_norm, hg_w_out, mla_w_down,
           mla_q_norm, mla_w_uq, mla_kv_norm, mla_w_ukv, mla_w_out):
    B, T, D = x.shape
    Lc = ctx.shape[1]
    depth = ada_down.shape[0]
    H = D // HEAD_DIM
    n_lat, n_ctx = B * T, B * Lc
    M = n_lat + n_ctx
    G = math.gcd(T, n_ctx)
    assert T % GLA_CHUNK == 0 and Lc % GLA_CHUNK == 0 and T % Lc == 0 and D % HEAD_DIM == 0

    cond = jnp.concatenate([c, c_ctx[None, :], jnp.zeros((7 - B % 8, D), F32)], axis=0)
    mods = ada_modulations(cond, ada_down, ada_up, ada_bias)
    group_src = np.concatenate([np.repeat(np.arange(B), T // G), np.full(n_ctx // G, B)])
    mods = mods[:, group_src, :].reshape(depth, M // G, 1, N_MOD, D)
    mod = lambda layer, idx: mods[layer, :, :, idx, :]

    lb = jnp.cumsum(jax.nn.softmax(hg_lb_logits.astype(F32), axis=0), axis=0)
    lb = lb - lb[:1]
    log_lb, log_1m_lb = jnp.log(lb), jnp.log1p(-lb)

    cos, sin = _rope_tables(B, T, Lc)
    xs = jnp.concatenate([x.reshape(n_lat, D), ctx.reshape(n_ctx, D)], axis=0)
    h = norm_modulate(xs, norm_mix_pre[0], mod(0, 0), mod(0, 1), G)

    for layer in range(depth):
        j = layer // 2
        if layer % 2 == 0:
            proj = hgrn2_project(h, hg_w_in[j].astype(BF16), log_lb[j], log_1m_lb[j])
            o = gla_scan(proj, batch=B, seq=T, ctx_len=Lc, reverse=False)
            o = gla_scan(proj, batch=B, seq=T, ctx_len=Lc, reverse=True, o_prev=o)
            r = hgrn2_readout_input(o, proj, hg_norm[j])
            y = matmul(r, hg_w_out[j].astype(BF16), out_dtype=F32, name="hgrn2_out")
        else:
            w_down = mla_w_down[j]
            n_lat_rank = MLA_Q_RANK + MLA_KV_RANK
            lat = matmul(h, w_down[:, :n_lat_rank].astype(BF16), out_dtype=F32, tn=512,
                         name="mla_down")
            kr = rope_matmul(h, _pad_heads(w_down[:, n_lat_rank:], 1, MLA_ROPE), cos, sin,
                             "mla_rope_key")
            cq, ckv = mla_latent_norms(lat, mla_q_norm[j], mla_kv_norm[j])
            w_uq = mla_w_uq[j].reshape(MLA_Q_RANK, H, HEAD_DIM + MLA_ROPE)
            qn = matmul(cq, w_uq[:, :, :HEAD_DIM].reshape(MLA_Q_RANK, D).astype(BF16),
                        out_dtype=BF16, name="mla_q_nope")
            qr = rope_matmul(cq, _pad_heads(w_uq[:, :, HEAD_DIM:].reshape(MLA_Q_RANK, -1), H,
                                            MLA_ROPE), cos, sin, "mla_q_rope")
            w_ukv = mla_w_ukv[j].reshape(MLA_KV_RANK, H, 2 * HEAD_DIM)
            kn = matmul(ckv, w_ukv[:, :, :HEAD_DIM].reshape(MLA_KV_RANK, D).astype(BF16),
                        out_dtype=BF16, name="mla_k_nope")
            v = matmul(ckv, w_ukv[:, :, HEAD_DIM:].reshape(MLA_KV_RANK, D).astype(BF16),
                       out_dtype=BF16, name="mla_value")
            o = mla_attention(qn, qr, kn, kr, v, batch=B, seq=T, ctx_len=Lc)
            y = matmul(o, mla_w_out[j].astype(BF16), out_dtype=F32, name="mla_out")
        xs, h = residual_norm(xs, y, norm_mix_post[layer], mod(layer, 2), G,
                              nxt=(norm_mlp_pre[layer], mod(layer, 3), mod(layer, 4)))
        hid = matmul(h, mlp_w1[layer].astype(BF16), out_dtype=BF16, epilogue=_epi_relu2,
                     name="mlp_up")
        m = matmul(hid, mlp_w2[layer].astype(BF16), out_dtype=F32, name="mlp_down")
        nxt = None
        if layer + 1 < depth:
            nxt = (norm_mix_pre[layer + 1], mod(layer + 1, 0), mod(layer + 1, 1))
        xs, h = residual_norm(xs, m, norm_mlp_post[layer], mod(layer, 5), G, nxt=nxt)
    return xs[:n_lat].reshape(B, T, D)
```

```python
import functools
import math

import numpy as np
import jax
import jax.numpy as jnp
from jax import lax
from jax.experimental import pallas as pl
from jax.experimental.pallas import tpu as pltpu

F32 = jnp.float32
BF16 = jnp.bfloat16

NORM_EPS = 1e-6
N_MOD = 6
GRID_W = 64
HEAD_DIM = 128
MLA_ROPE = 64
MLA_Q_RANK = 1024
MLA_KV_RANK = 512
MLA_SCALE = (HEAD_DIM + MLA_ROPE) ** -0.5
ATTN_EXP2_SCALE = MLA_SCALE * math.log2(math.e)
ATTN_Q_TILE = 1024
ATTN_KEY_CHUNK = 1024
ROPE_THETA = 10000.0
GLA_CHUNK = 64
GLA_LEVELS = 6

V7X_LANES = 128
V7X_VMEM_LIMIT = 48 * 1024 * 1024

MM_FULL_K_MAX = 4096
MM_SPLIT_K = 2048

_NT = (((1,), (1,)), ((), ()))


def _tile(dim, pref, align=V7X_LANES):
    if dim <= pref:
        return dim
    t = (pref // align) * align
    while t >= align:
        if dim % t == 0:
            return t
        t -= align
    return dim


def _params(sem):
    return pltpu.CompilerParams(dimension_semantics=sem, vmem_limit_bytes=V7X_VMEM_LIMIT)


def _sigmoid(x):
    return 1.0 / (1.0 + jnp.exp(-x))


def _split3(x):
    hi = x.astype(BF16)
    r = x - hi.astype(F32)
    mid = r.astype(BF16)
    lo = (r - mid.astype(F32)).astype(BF16)
    return hi, mid, lo


def _mm_body(a_ref, b_ref, *rest, nk, n_aux, epilogue):
    aux = rest[:n_aux]
    o_ref = rest[n_aux]
    if nk == 1:
        epilogue(jnp.dot(a_ref[...], b_ref[...], preferred_element_type=F32), o_ref, *aux)
        return
    acc_ref = rest[n_aux + 1]
    k = pl.program_id(2)

    @pl.when(k == 0)
    def _():
        acc_ref[...] = jnp.dot(a_ref[...], b_ref[...], preferred_element_type=F32)

    @pl.when(k > 0)
    def _():
        acc_ref[...] += jnp.dot(a_ref[...], b_ref[...], preferred_element_type=F32)

    @pl.when(k == nk - 1)
    def _():
        epilogue(acc_ref[...], o_ref, *aux)


def _epi_plain(acc, o_ref):
    o_ref[...] = acc.astype(o_ref.dtype)


def _epi_relu2(acc, o_ref):
    r = jnp.maximum(acc, 0.0)
    o_ref[...] = (r * r).astype(o_ref.dtype)


def matmul(a, b, *, out_dtype, epilogue=_epi_plain, aux=(), out_shape=None, out_spec=None,
           tm=1024, tn=1024, tk=None, name="mm"):
    M, K = a.shape
    _, N = b.shape
    if tk is None:
        tk = K if K <= MM_FULL_K_MAX else MM_SPLIT_K
    tm, tn, tk = _tile(M, tm, 8), _tile(N, tn), _tile(K, tk)
    nk = K // tk
    if out_shape is None:
        out_shape = jax.ShapeDtypeStruct((M, N), out_dtype)
        out_spec = pl.BlockSpec((tm, tn), lambda i, j, k: (i, j))
    in_specs = [pl.BlockSpec((tm, tk), lambda i, j, k: (i, k)),
                pl.BlockSpec((tk, tn), lambda i, j, k: (k, j))]
    in_specs += [spec for _, spec in aux]
    scratch = [] if nk == 1 else [pltpu.VMEM((tm, tn), F32)]
    return pl.pallas_call(
        functools.partial(_mm_body, nk=nk, n_aux=len(aux), epilogue=epilogue),
        grid=(M // tm, N // tn, nk),
        in_specs=in_specs,
        out_specs=out_spec,
        out_shape=out_shape,
        scratch_shapes=scratch,
        compiler_params=_params(("parallel", "parallel", "arbitrary")),
        name=name,
    )(a, b, *[arr for arr, _ in aux])


def _dot3(a, b):
    a0, a1, a2 = _split3(a)
    b0, b1, b2 = _split3(b)
    d = lambda x, y: jnp.dot(x, y, preferred_element_type=F32)
    return ((d(a0, b0) + (d(a0, b1) + d(a1, b0)))
            + ((d(a0, b2) + d(a2, b0)) + d(a1, b1)))


def _ada_body(cond_ref, down_ref, up_ref, bias_ref, o_ref):
    cond = cond_ref[...]
    s = cond * _sigmoid(cond)
    t = _dot3(s, down_ref[...])
    o_ref[...] = _dot3(t, up_ref[...]) + bias_ref[...]


def ada_modulations(cond, ada_down, ada_up, ada_bias):
    depth, D, rank = ada_down.shape
    R = cond.shape[0]
    N = ada_up.shape[2]
    tn = _tile(N, 2048)
    return pl.pallas_call(
        _ada_body,
        grid=(depth, N // tn),
        in_specs=[pl.BlockSpec((R, D), lambda l, j: (0, 0)),
                  pl.BlockSpec((None, D, rank), lambda l, j: (l, 0, 0)),
                  pl.BlockSpec((None, rank, tn), lambda l, j: (l, 0, j)),
                  pl.BlockSpec((None, 1, tn), lambda l, j: (l, 0, j))],
        out_specs=pl.BlockSpec((None, R, tn), lambda l, j: (l, 0, j)),
        out_shape=jax.ShapeDtypeStruct((depth, R, N), F32),
        compiler_params=_params(("parallel", "arbitrary")),
        name="ada_mod",
    )(cond, ada_down, ada_up, ada_bias.reshape(depth, 1, N))


def _rms(x, gain):
    ms = jnp.mean(x * x, axis=-1, keepdims=True)
    return x * lax.rsqrt(ms + NORM_EPS) * gain


def _pre_body(x_ref, g_ref, sh_ref, sc_ref, h_ref):
    h = _rms(x_ref[...], g_ref[...])
    h_ref[...] = (h * (1.0 + sc_ref[...]) + sh_ref[...]).astype(h_ref.dtype)


def norm_modulate(x, gain, shift, scale, group):
    M, D = x.shape
    tr = _tile(group, 256, 8)
    per = group // tr
    row = pl.BlockSpec((tr, D), lambda i: (i, 0))
    vec = pl.BlockSpec((1, D), lambda i: (0, 0))
    mod = pl.BlockSpec((None, 1, D), lambda i: (i // per, 0, 0))
    return pl.pallas_call(
        _pre_body,
        grid=(M // tr,),
        in_specs=[row, vec, mod, mod],
        out_specs=row,
        out_shape=jax.ShapeDtypeStruct((M, D), BF16),
        compiler_params=_params(("parallel",)),
        name="norm_modulate",
    )(x, gain.reshape(1, D), shift, scale)


def _post_body(x_ref, y_ref, gpost_ref, gate_ref, *rest, with_next):
    x = x_ref[...] + gate_ref[...] * _rms(y_ref[...], gpost_ref[...])
    if with_next:
        gpre_ref, sh_ref, sc_ref, xo_ref, h_ref = rest
        xo_ref[...] = x
        h = _rms(x, gpre_ref[...])
        h_ref[...] = (h * (1.0 + sc_ref[...]) + sh_ref[...]).astype(h_ref.dtype)
    else:
        (xo_ref,) = rest
        xo_ref[...] = x


def residual_norm(x, y, g_post, gate, group, nxt=None):
    M, D = x.shape
    tr = _tile(group, 256, 8)
    per = group // tr
    row = pl.BlockSpec((tr, D), lambda i: (i, 0))
    vec = pl.BlockSpec((1, D), lambda i: (0, 0))
    mod = pl.BlockSpec((None, 1, D), lambda i: (i // per, 0, 0))
    args = [x, y, g_post.reshape(1, D), gate]
    in_specs = [row, row, vec, mod]
    out_shape = [jax.ShapeDtypeStruct((M, D), F32)]
    out_specs = [row]
    if nxt is not None:
        gain, shift, scale = nxt
        args += [gain.reshape(1, D), shift, scale]
        in_specs += [vec, mod, mod]
        out_shape.append(jax.ShapeDtypeStruct((M, D), BF16))
        out_specs.append(row)
    out = pl.pallas_call(
        functools.partial(_post_body, with_next=nxt is not None),
        grid=(M // tr,),
        in_specs=in_specs,
        out_specs=out_specs,
        out_shape=out_shape,
        input_output_aliases={0: 0},
        compiler_params=_params(("parallel",)),
        name="residual_norm",
    )(*args)
    return (out[0], out[1]) if nxt is not None else (out[0], None)


def _readout_body(o_ref, g_ref, gain_ref, h_ref):
    g = g_ref[...]
    h_ref[...] = (_rms(o_ref[...], gain_ref[...]) * (g * _sigmoid(g))).astype(h_ref.dtype)


def hgrn2_readout_input(o, proj, gain):
    M, D = o.shape
    tr = _tile(M, 256, 8)
    row = pl.BlockSpec((tr, D), lambda i: (i, 0))
    return pl.pallas_call(
        _readout_body,
        grid=(M // tr,),
        in_specs=[row, pl.BlockSpec((None, tr, D), lambda i: (4, i, 0)),
                  pl.BlockSpec((1, D), lambda i: (0, 0))],
        out_specs=row,
        out_shape=jax.ShapeDtypeStruct((M, D), BF16),
        compiler_params=_params(("parallel",)),
        name="hgrn2_readout_input",
    )(o, proj, gain.reshape(1, D))


def _latent_norm_body(c_ref, qg_ref, kvg_ref, cq_ref, ckv_ref):
    c = c_ref[...]
    cq_ref[...] = _rms(c[:, :MLA_Q_RANK], qg_ref[...]).astype(cq_ref.dtype)
    ckv_ref[...] = _rms(c[:, MLA_Q_RANK:], kvg_ref[...]).astype(ckv_ref.dtype)


def mla_latent_norms(c, q_gain, kv_gain):
    M, W = c.shape
    tr = _tile(M, 512, 8)
    return pl.pallas_call(
        _latent_norm_body,
        grid=(M // tr,),
        in_specs=[pl.BlockSpec((tr, W), lambda i: (i, 0)),
                  pl.BlockSpec((1, MLA_Q_RANK), lambda i: (0, 0)),
                  pl.BlockSpec((1, MLA_KV_RANK), lambda i: (0, 0))],
        out_specs=[pl.BlockSpec((tr, MLA_Q_RANK), lambda i: (i, 0)),
                   pl.BlockSpec((tr, MLA_KV_RANK), lambda i: (i, 0))],
        out_shape=[jax.ShapeDtypeStruct((M, MLA_Q_RANK), BF16),
                   jax.ShapeDtypeStruct((M, MLA_KV_RANK), BF16)],
        compiler_params=_params(("parallel",)),
        name="mla_latent_norms",
    )(c, q_gain.reshape(1, -1), kv_gain.reshape(1, -1))


def _hg_proj_epilogue(acc, o_ref, llb_ref, l1m_ref, *, tiles_per_section):
    sec = pl.program_id(1) // tiles_per_section

    @pl.when(sec == 0)
    def _():
        o_ref[...] = acc * _sigmoid(acc)

    @pl.when(jnp.logical_or(sec == 1, sec == 4))
    def _():
        o_ref[...] = acc

    @pl.when(jnp.logical_or(sec == 2, sec == 3))
    def _():
        log_sig = jnp.minimum(acc, 0.0) - jnp.log1p(jnp.exp(-jnp.abs(acc)))
        a = llb_ref[...]
        b = l1m_ref[...] + log_sig
        o_ref[...] = jnp.maximum(a, b) + jnp.log1p(jnp.exp(-jnp.abs(a - b)))


def hgrn2_project(h, w_in, log_lb, log_1m_lb):
    M, D = h.shape
    tm, tn = _tile(M, 1024, 8), _tile(D, 512)
    per = D // tn
    zeros = jnp.zeros((1, D), F32)
    aux_a = jnp.concatenate([zeros, zeros, log_lb, zeros], axis=0).reshape(5, 1, D)
    aux_b = jnp.concatenate([zeros, zeros, log_1m_lb, zeros], axis=0).reshape(5, 1, D)
    aux_spec = pl.BlockSpec((None, 1, tn), lambda i, j, k: (j // per, 0, j % per))
    return matmul(
        h, w_in, out_dtype=F32,
        epilogue=functools.partial(_hg_proj_epilogue, tiles_per_section=per),
        aux=((aux_a, aux_spec), (aux_b, aux_spec)),
        out_shape=jax.ShapeDtypeStruct((5, M, D), F32),
        out_spec=pl.BlockSpec((None, tm, tn), lambda i, j, k: (j // per, i, j % per)),
        tm=tm, tn=tn, name="hgrn2_project")


def _gla_tables(reverse):
    C, L = GLA_CHUNK, GLA_LEVELS
    r = np.arange(C)
    mats = []
    if not reverse:
        mats.append((r[None, :] <= r[:, None]))
        mats.append((r[None, :] > r[:, None]))
    else:
        mats.append((r[None, :] >= r[:, None]))
        mats.append((r[None, :] < r[:, None]))
    lvl = np.full((C, C), -1, np.int32)
    lvl[r, r] = L
    for l in range(L):
        w = C >> (l + 1)
        blk = r // w
        odd = (blk % 2) == 1
        rho = ((blk // 2) * 2 + 1) * w
        m = np.zeros((C, C), bool)
        for t in range(C):
            if not reverse:
                cols = (r >= rho[t]) & (r <= t) if odd[t] else (r > t) & (r < rho[t])
            else:
                cols = (r >= rho[t]) & (r < t) if odd[t] else (r >= t) & (r < rho[t])
            m[t] = cols
        mats.append(m)
        same_parent = (r[:, None] // (2 * w)) == (r[None, :] // (2 * w))
        if not reverse:
            pair = same_parent & odd[:, None] & ~odd[None, :]
        else:
            pair = same_parent & ~odd[:, None] & odd[None, :]
        lvl[pair] = l
    tail = np.zeros((16, C), bool)
    tail[0] = True
    mats.append(tail)
    m = np.concatenate(mats, axis=0).astype(np.float32)
    return np.concatenate([m, m, m], axis=1), lvl


def _gla_body(q_ref, v_ref, g_ref, mcat_ref, lvl_ref, *rest, heads, accumulate):
    if accumulate:
        oprev_ref, o_ref, st_ref, ex_ref = rest
    else:
        o_ref, st_ref, ex_ref = rest
    C, L = GLA_CHUNK, GLA_LEVELS

    @pl.when(pl.program_id(2) == 0)
    def _():
        st_ref[...] = jnp.zeros_like(st_ref)

    g = g_ref[...]
    ex_ref[...] = jnp.dot(mcat_ref[...], jnp.concatenate(_split3(g), axis=0),
                          preferred_element_type=F32)
    lvl = lvl_ref[...]
    masks = [lvl == l for l in range(L + 1)]
    hs = [slice(h * HEAD_DIM, (h + 1) * HEAD_DIM) for h in range(heads)]
    dot_nt = lambda a, b: lax.dot_general(a, b, _NT, preferred_element_type=F32)

    q = [q_ref[:, sl] for sl in hs]
    k = [1.0 - jnp.exp(g[:, sl]) for sl in hs]
    st = [st_ref[h] for h in range(heads)]
    o = [dot_nt((q[h] * jnp.exp(ex_ref[0:C, hs[h]])).astype(BF16), st[h].astype(BF16))
         for h in range(heads)]
    att = [dot_nt(q[h].astype(BF16), k[h].astype(BF16)) for h in range(heads)]
    att = [jnp.where(masks[L], a, 0.0) for a in att]
    for l in range(L):
        e = [jnp.exp(ex_ref[(2 + l) * C:(3 + l) * C, sl]) for sl in hs]
        p = [dot_nt((q[h] * e[h]).astype(BF16), (k[h] * e[h]).astype(BF16)) for h in range(heads)]
        att = [jnp.where(masks[l], p[h], att[h]) for h in range(heads)]
    v = [v_ref[:, sl] for sl in hs]
    for h in range(heads):
        oh = o[h] + jnp.dot(att[h].astype(BF16), v[h].astype(BF16), preferred_element_type=F32)
        if accumulate:
            oh = oh + oprev_ref[:, hs[h]]
        o_ref[:, hs[h]] = oh
    for h in range(heads):
        ke = (k[h] * jnp.exp(ex_ref[C:2 * C, hs[h]])).astype(BF16)
        decay = jnp.exp(ex_ref[(2 + L) * C:(2 + L) * C + 1, hs[h]])
        st_ref[h] = decay * st[h] + jnp.dot(v[h].T.astype(BF16), ke, preferred_element_type=F32)


def gla_scan(proj, *, batch, seq, ctx_len, reverse, o_prev=None):
    _, M, D = proj.shape
    C = GLA_CHUNK
    heads = min(8, D // HEAD_DIM)
    width = heads * HEAD_DIM
    n_ctx, n_lat = ctx_len // C, seq // C
    ctx_base = batch * seq // C
    mcat, lvl = _gla_tables(reverse)
    mcat = jnp.asarray(mcat, BF16)
    lvl = jnp.asarray(lvl)
    g_section = 3 if reverse else 2

    def row_block(b, n):
        if reverse:
            return jnp.where(n < n_ctx, ctx_base + b * n_ctx + (n_ctx - 1 - n),
                             b * n_lat + (n_lat - 1 - (n - n_ctx)))
        return jnp.where(n < n_ctx, ctx_base + b * n_ctx + n, b * n_lat + (n - n_ctx))

    def section(s):
        return pl.BlockSpec((None, C, width), lambda b, hg, n: (s, row_block(b, n), hg))

    row = pl.BlockSpec((C, width), lambda b, hg, n: (row_block(b, n), hg))
    const = lambda shape: pl.BlockSpec(shape, lambda b, hg, n: (0, 0))
    args = [proj, proj, proj, mcat, lvl]
    in_specs = [section(0), section(1), section(g_section), const(mcat.shape), const(lvl.shape)]
    aliases = {}
    if o_prev is not None:
        args.append(o_prev)
        in_specs.append(row)
        aliases = {5: 0}
    return pl.pallas_call(
        functools.partial(_gla_body, heads=heads, accumulate=o_prev is not None),
        grid=(batch, D // width, n_ctx + n_lat),
        in_specs=in_specs,
        out_specs=row,
        out_shape=jax.ShapeDtypeStruct((M, D), F32),
        scratch_shapes=[pltpu.VMEM((heads, HEAD_DIM, HEAD_DIM), F32),
                        pltpu.VMEM(mcat.shape[:1] + (width,), F32)],
        input_output_aliases=aliases,
        compiler_params=_params(("parallel", "parallel", "arbitrary")),
        name="gla_bwd" if reverse else "gla_fwd",
    )(*args)


def _rope_epilogue(acc, o_ref, cos_ref, sin_ref):
    cos, sin = cos_ref[...], sin_ref[...]
    lane = lax.broadcasted_iota(jnp.int32, cos.shape, 1)
    first_half = (lane % 32) < 16
    for hh in range(acc.shape[1] // V7X_LANES):
        sl = slice(hh * V7X_LANES, (hh + 1) * V7X_LANES)
        x = acc[:, sl]
        rot = jnp.where(first_half, -pltpu.roll(x, V7X_LANES - 16, 1), pltpu.roll(x, 16, 1))
        o_ref[:, sl] = (x * cos + rot * sin).astype(o_ref.dtype)


def rope_matmul(a, w, cos, sin, name):
    M = a.shape[0]
    tm = _tile(M, 1024, 8)
    tab = pl.BlockSpec((tm, V7X_LANES), lambda i, j, k: (i, 0))
    return matmul(a, w, out_dtype=BF16, epilogue=_rope_epilogue,
                  aux=((cos, tab), (sin, tab)), tm=tm, name=name)


def _attn_latent_body(qn_ref, qr_ref, knc_ref, krc_ref, vc_ref, knl_ref, krl_ref, vl_ref, o_ref,
                      kcat_ref, *, ctx_len, chunk):
    @pl.when(pl.program_id(2) == 0)
    def _():
        kcat_ref[0:ctx_len, 0:HEAD_DIM] = knc_ref[...]
        kcat_ref[0:ctx_len, HEAD_DIM:] = krc_ref[...]
        kcat_ref[ctx_len:, 0:HEAD_DIM] = knl_ref[...]
        kcat_ref[ctx_len:, HEAD_DIM:] = krl_ref[...]

    q = jnp.concatenate([qn_ref[...], qr_ref[...]], axis=1)
    seq = vl_ref.shape[0]
    spans = [(0, ctx_len)] + [(ctx_len + c, chunk) for c in range(0, seq, chunk)]
    m = l = acc = None
    for start, size in spans:
        v = vc_ref[...] if start == 0 else vl_ref[start - ctx_len:start - ctx_len + size, :]
        s = lax.dot_general(q, kcat_ref[start:start + size, :], _NT, preferred_element_type=F32)
        m_c = jnp.max(s, axis=-1, keepdims=True)
        m_new = m_c if m is None else jnp.maximum(m, m_c)
        p = jnp.exp2((s - m_new) * ATTN_EXP2_SCALE)
        row_sum = jnp.sum(p, axis=-1, keepdims=True)
        pv = jnp.dot(p.astype(BF16), v, preferred_element_type=F32)
        if m is None:
            l, acc = row_sum, pv
        else:
            alpha = jnp.exp2((m - m_new) * ATTN_EXP2_SCALE)
            l = alpha * l + row_sum
            acc = alpha * acc + pv
        m = m_new
    o_ref[...] = (acc / l).astype(o_ref.dtype)


def _attn_context_body(qn_ref, qr_ref, kn_ref, kr_ref, v_ref, alias_ref, o_ref):
    del alias_ref
    s = (lax.dot_general(qn_ref[...], kn_ref[...], _NT, preferred_element_type=F32)
         + lax.dot_general(qr_ref[...], kr_ref[...], _NT, preferred_element_type=F32))
    m = jnp.max(s, axis=-1, keepdims=True)
    p = jnp.exp2((s - m) * ATTN_EXP2_SCALE)
    o = jnp.dot(p.astype(BF16), v_ref[...], preferred_element_type=F32)
    o_ref[...] = (o / jnp.sum(p, axis=-1, keepdims=True)).astype(o_ref.dtype)


def mla_attention(qn, qr, kn, kr, v, *, batch, seq, ctx_len):
    M, D = qn.shape
    H = D // HEAD_DIM
    tq = _tile(seq, ATTN_Q_TILE, 8)
    nq = seq // tq
    chunk = _tile(seq, ATTN_KEY_CHUNK, 8)
    ctx_base = batch * seq // ctx_len
    blk = lambda rows, fn: pl.BlockSpec((rows, HEAD_DIM), fn)
    ctx_keys = [blk(ctx_len, lambda b, h, i: (ctx_base + b, h)),
                blk(ctx_len, lambda b, h, i: (ctx_base + b, 0)),
                blk(ctx_len, lambda b, h, i: (ctx_base + b, h))]
    lat_keys = [blk(seq, lambda b, h, i: (b, h)),
                blk(seq, lambda b, h, i: (b, 0)),
                blk(seq, lambda b, h, i: (b, h))]
    q_lat = blk(tq, lambda b, h, i: (b * nq + i, h))
    o_lat = pl.pallas_call(
        functools.partial(_attn_latent_body, ctx_len=ctx_len, chunk=chunk),
        grid=(batch, H, nq),
        in_specs=[q_lat, q_lat] + ctx_keys + lat_keys,
        out_specs=q_lat,
        out_shape=jax.ShapeDtypeStruct((M, D), BF16),
        scratch_shapes=[pltpu.VMEM((ctx_len + seq, 2 * HEAD_DIM), BF16)],
        compiler_params=_params(("parallel", "parallel", "arbitrary")),
        name="mla_attention_latent",
    )(qn, qr, kn, kr, v, kn, kr, v)
    q_ctx = blk(ctx_len, lambda b, h, i: (ctx_base + b, h))
    return pl.pallas_call(
        _attn_context_body,
        grid=(batch, H, 1),
        in_specs=[q_ctx, q_ctx] + ctx_keys + [pl.BlockSpec(memory_space=pl.ANY)],
        out_specs=q_ctx,
        out_shape=jax.ShapeDtypeStruct((M, D), BF16),
        input_output_aliases={5: 0},
        compiler_params=_params(("parallel", "parallel", "arbitrary")),
        name="mla_attention_context",
    )(qn, qr, kn, kr, v, o_lat)


def _rope_tables(batch, seq, ctx_len):
    t = jnp.arange(seq)
    row = (t // GRID_W).astype(F32)
    col = (t % GRID_W).astype(F32)
    axis_dim = MLA_ROPE // 2
    inv_freq = ROPE_THETA ** (-jnp.arange(0, axis_dim, 2, dtype=F32) / axis_dim)
    ang_r = row[:, None] * inv_freq
    ang_c = col[:, None] * inv_freq
    ang = jnp.concatenate([ang_r, ang_r, ang_c, ang_c], axis=-1)
    pad = V7X_LANES - MLA_ROPE
    cos = jnp.pad(jnp.cos(ang), ((0, 0), (0, pad)), constant_values=1.0)
    sin = jnp.pad(jnp.sin(ang), ((0, 0), (0, pad)))
    n_ctx = batch * ctx_len
    cos = jnp.concatenate([jnp.tile(cos, (batch, 1)), jnp.ones((n_ctx, V7X_LANES), F32)], axis=0)
    sin = jnp.concatenate([jnp.tile(sin, (batch, 1)), jnp.zeros((n_ctx, V7X_LANES), F32)], axis=0)
    return cos, sin


def _pad_heads(w, heads, width):
    K = w.shape[0]
    w = w.reshape(K, heads, width)
    w = jnp.pad(w, ((0, 0), (0, 0), (0, V7X_LANES - width)))
    return w.reshape(K, heads * V7X_LANES).astype(BF16)


def kernel(x, c, ctx, c_ctx, ada_down, ada_up, ada_bias, norm_mix_pre, norm_mix_post, norm_mlp_pre,
           norm_mlp_post, mlp_w1, mlp_w2, hg_w_in, hg_lb_logits, hg_norm, hg_w_out, mla_w_down,
           mla_q_norm, mla_w_uq, mla_kv_norm, mla_w_ukv, mla_w_out):
    B, T, D = x.shape
    Lc = ctx.shape[1]
    depth = ada_down.shape[0]
    H = D // HEAD_DIM
    n_lat, n_ctx = B * T, B * Lc
    M = n_lat + n_ctx
    G = math.gcd(T, n_ctx)
    assert T % GLA_CHUNK == 0 and Lc % GLA_CHUNK == 0 and T % Lc == 0 and D % HEAD_DIM == 0

    cond = jnp.concatenate([c, c_ctx[None, :], jnp.zeros((7 - B % 8, D), F32)], axis=0)
    mods = ada_modulations(cond, ada_down, ada_up, ada_bias)
    group_src = np.concatenate([np.repeat(np.arange(B), T // G), np.full(n_ctx // G, B)])
    mods = mods[:, group_src, :].reshape(depth, M // G, 1, N_MOD, D)
    mod = lambda layer, idx: mods[layer, :, :, idx, :]

    lb = jnp.cumsum(jax.nn.softmax(hg_lb_logits.astype(F32), axis=0), axis=0)
    lb = lb - lb[:1]
    log_lb, log_1m_lb = jnp.log(lb), jnp.log1p(-lb)

    cos, sin = _rope_tables(B, T, Lc)
    xs = jnp.concatenate([x.reshape(n_lat, D), ctx.reshape(n_ctx, D)], axis=0)
    h = norm_modulate(xs, norm_mix_pre[0], mod(0, 0), mod(0, 1), G)

    for layer in range(depth):
        j = layer // 2
        if layer % 2 == 0:
            proj = hgrn2_project(h, hg_w_in[j].astype(BF16), log_lb[j], log_1m_lb[j])
            o = gla_scan(proj, batch=B, seq=T, ctx_len=Lc, reverse=False)
            o = gla_scan(proj, batch=B, seq=T, ctx_len=Lc, reverse=True, o_prev=o)
            r = hgrn2_readout_input(o, proj, hg_norm[j])
            y = matmul(r, hg_w_out[j].astype(BF16), out_dtype=F32, name="hgrn2_out")
        else:
            w_down = mla_w_down[j]
            n_lat_rank = MLA_Q_RANK + MLA_KV_RANK
            lat = matmul(h, w_down[:, :n_lat_rank].astype(BF16), out_dtype=F32, tn=512,
                         name="mla_down")
            kr = rope_matmul(h, _pad_heads(w_down[:, n_lat_rank:], 1, MLA_ROPE), cos, sin,
                             "mla_rope_key")
            cq, ckv = mla_latent_norms(lat, mla_q_norm[j], mla_kv_norm[j])
            w_uq = mla_w_uq[j].reshape(MLA_Q_RANK, H, HEAD_DIM + MLA_ROPE)
            qn = matmul(cq, w_uq[:, :, :HEAD_DIM].reshape(MLA_Q_RANK, D).astype(BF16),
                        out_dtype=BF16, name="mla_q_nope")
            qr = rope_matmul(cq, _pad_heads(w_uq[:, :, HEAD_DIM:].reshape(MLA_Q_RANK, -1), H,
                                            MLA_ROPE), cos, sin, "mla_q_rope")
            w_ukv = mla_w_ukv[j].reshape(MLA_KV_RANK, H, 2 * HEAD_DIM)
            kn = matmul(ckv, w_ukv[:, :, :HEAD_DIM].reshape(MLA_KV_RANK, D).astype(BF16),
                        out_dtype=BF16, name="mla_k_nope")
            v = matmul(ckv, w_ukv[:, :, HEAD_DIM:].reshape(MLA_KV_RANK, D).astype(BF16),
                       out_dtype=BF16, name="mla_value")
            o = mla_attention(qn, qr, kn, kr, v, batch=B, seq=T, ctx_len=Lc)
            y = matmul(o, mla_w_out[j].astype(BF16), out_dtype=F32, name="mla_out")
        xs, h = residual_norm(xs, y, norm_mix_post[layer], mod(layer, 2), G,
                              nxt=(norm_mlp_pre[layer], mod(layer, 3), mod(layer, 4)))
        hid = matmul(h, mlp_w1[layer].astype(BF16), out_dtype=BF16, epilogue=_epi_relu2,
                     name="mlp_up")
        m = matmul(hid, mlp_w2[layer].astype(BF16), out_dtype=F32, name="mlp_down")
        nxt = None
        if layer + 1 < depth:
            nxt = (norm_mix_pre[layer + 1], mod(layer + 1, 0), mod(layer + 1, 1))
        xs, h = residual_norm(xs, m, norm_mlp_post[layer], mod(layer, 5), G, nxt=nxt)
    return xs[:n_lat].reshape(B, T, D)
```

```python
import functools
import math

import numpy as np
import jax
import jax.numpy as jnp
from jax import lax
from jax.experimental import pallas as pl
from jax.experimental.pallas import tpu as pltpu

F32 = jnp.float32
BF16 = jnp.bfloat16

NORM_EPS = 1e-6
N_MOD = 6
GRID_W = 64
HEAD_DIM = 128
MLA_ROPE = 64
MLA_Q_RANK = 1024
MLA_KV_RANK = 512
MLA_SCALE = (HEAD_DIM + MLA_ROPE) ** -0.5
ATTN_EXP2_SCALE = MLA_SCALE * math.log2(math.e)
ATTN_Q_TILE = 2048
ATTN_KEY_CHUNK = 1024
ROPE_THETA = 10000.0
GLA_CHUNK = 64
GLA_HEADS_PER_STEP = 32
GLA_LEVELS = 6

V7X_LANES = 128
V7X_VMEM_LIMIT = 56 * 1024 * 1024

MM_FULL_K_MAX = 4096
MM_SPLIT_K = 4096

_NT = (((1,), (1,)), ((), ()))


def _tile(dim, pref, align=V7X_LANES):
    if dim <= pref:
        return dim
    t = (pref // align) * align
    while t >= align:
        if dim % t == 0:
            return t
        t -= align
    return dim


def _params(sem):
    return pltpu.CompilerParams(dimension_semantics=sem, vmem_limit_bytes=V7X_VMEM_LIMIT)


def _sigmoid(x):
    return 1.0 / (1.0 + jnp.exp(-x))


def _split3(x):
    hi = x.astype(BF16)
    r = x - hi.astype(F32)
    mid = r.astype(BF16)
    lo = (r - mid.astype(F32)).astype(BF16)
    return hi, mid, lo


def _mm_body(a_ref, b_ref, *rest, nk, n_aux, epilogue):
    aux = rest[:n_aux]
    o_ref = rest[n_aux]
    if nk == 1:
        epilogue(jnp.dot(a_ref[...], b_ref[...], preferred_element_type=F32), o_ref, *aux)
        return
    acc_ref = rest[n_aux + 1]
    k = pl.program_id(2)

    @pl.when(k == 0)
    def _():
        acc_ref[...] = jnp.dot(a_ref[...], b_ref[...], preferred_element_type=F32)

    @pl.when(k > 0)
    def _():
        acc_ref[...] += jnp.dot(a_ref[...], b_ref[...], preferred_element_type=F32)

    @pl.when(k == nk - 1)
    def _():
        epilogue(acc_ref[...], o_ref, *aux)


def _epi_plain(acc, o_ref):
    o_ref[...] = acc.astype(o_ref.dtype)


def _epi_relu2(acc, o_ref):
    r = jnp.maximum(acc, 0.0)
    o_ref[...] = (r * r).astype(o_ref.dtype)


def matmul(a, b, *, out_dtype, epilogue=_epi_plain, aux=(), out_shape=None, out_spec=None,
           tm=1024, tn=1024, tk=None, rows=None, name="mm"):
    K = a.shape[1]
    M = a.shape[0] if rows is None else rows
    _, N = b.shape
    if tk is None:
        tk = K if K <= MM_FULL_K_MAX else MM_SPLIT_K
    tm, tn, tk = _tile(M, tm, 8), _tile(N, tn), _tile(K, tk)
    nk = K // tk
    if out_shape is None:
        out_shape = jax.ShapeDtypeStruct((M, N), out_dtype)
        out_spec = pl.BlockSpec((tm, tn), lambda i, j, k: (i, j))
    in_specs = [pl.BlockSpec((tm, tk), lambda i, j, k: (i, k)),
                pl.BlockSpec((tk, tn), lambda i, j, k: (k, j))]
    in_specs += [spec for _, spec in aux]
    scratch = [] if nk == 1 else [pltpu.VMEM((tm, tn), F32)]
    return pl.pallas_call(
        functools.partial(_mm_body, nk=nk, n_aux=len(aux), epilogue=epilogue),
        grid=(M // tm, N // tn, nk),
        in_specs=in_specs,
        out_specs=out_spec,
        out_shape=out_shape,
        scratch_shapes=scratch,
        compiler_params=_params(("parallel", "parallel", "arbitrary")),
        name=name,
    )(a, b, *[arr for arr, _ in aux])


def _dot3(a, b):
    a0, a1, a2 = _split3(a)
    b0, b1, b2 = _split3(b)
    d = lambda x, y: jnp.dot(x, y, preferred_element_type=F32)
    return ((d(a0, b0) + (d(a0, b1) + d(a1, b0)))
            + ((d(a0, b2) + d(a2, b0)) + d(a1, b1)))


def _ada_body(cond_ref, down_ref, up_ref, bias_ref, o_ref):
    cond = cond_ref[...]
    s = cond * _sigmoid(cond)
    t = _dot3(s, down_ref[...])
    o_ref[...] = _dot3(t, up_ref[...]) + bias_ref[...]


def ada_modulations(cond, ada_down, ada_up, ada_bias):
    depth, D, rank = ada_down.shape
    R = cond.shape[0]
    N = ada_up.shape[2]
    tn = _tile(N, 2048)
    return pl.pallas_call(
        _ada_body,
        grid=(depth, N // tn),
        in_specs=[pl.BlockSpec((R, D), lambda l, j: (0, 0)),
                  pl.BlockSpec((None, D, rank), lambda l, j: (l, 0, 0)),
                  pl.BlockSpec((None, rank, tn), lambda l, j: (l, 0, j)),
                  pl.BlockSpec((None, 1, tn), lambda l, j: (l, 0, j))],
        out_specs=pl.BlockSpec((None, R, tn), lambda l, j: (l, 0, j)),
        out_shape=jax.ShapeDtypeStruct((depth, R, N), F32),
        compiler_params=_params(("parallel", "arbitrary")),
        name="ada_mod",
    )(cond, ada_down, ada_up, ada_bias.reshape(depth, 1, N))


def _rms(x, gain):
    ms = jnp.mean(x * x, axis=-1, keepdims=True)
    return x * lax.rsqrt(ms + NORM_EPS) * gain


def _pre_body(x_ref, g_ref, sh_ref, sc_ref, h_ref):
    h = _rms(x_ref[...], g_ref[...])
    h_ref[...] = (h * (1.0 + sc_ref[...]) + sh_ref[...]).astype(h_ref.dtype)


def norm_modulate(x, gain, shift, scale, group):
    M, D = x.shape
    tr = _tile(group, 256, 8)
    per = group // tr
    row = pl.BlockSpec((tr, D), lambda i: (i, 0))
    vec = pl.BlockSpec((1, D), lambda i: (0, 0))
    mod = pl.BlockSpec((None, 1, D), lambda i: (i // per, 0, 0))
    return pl.pallas_call(
        _pre_body,
        grid=(M // tr,),
        in_specs=[row, vec, mod, mod],
        out_specs=row,
        out_shape=jax.ShapeDtypeStruct((M, D), BF16),
        compiler_params=_params(("parallel",)),
        name="norm_modulate",
    )(x, gain.reshape(1, D), shift, scale)


def _post_body(x_ref, y_ref, gpost_ref, gate_ref, *rest, with_next):
    x = x_ref[...] + gate_ref[...] * _rms(y_ref[...], gpost_ref[...])
    if with_next:
        gpre_ref, sh_ref, sc_ref, xo_ref, h_ref = rest
        xo_ref[...] = x
        h = _rms(x, gpre_ref[...])
        h_ref[...] = (h * (1.0 + sc_ref[...]) + sh_ref[...]).astype(h_ref.dtype)
    else:
        (xo_ref,) = rest
        xo_ref[...] = x


def residual_norm(x, y, g_post, gate, group, nxt=None):
    M, D = y.shape
    tr = _tile(group, 256, 8)
    per = group // tr
    row = pl.BlockSpec((tr, D), lambda i: (i, 0))
    vec = pl.BlockSpec((1, D), lambda i: (0, 0))
    mod = pl.BlockSpec((None, 1, D), lambda i: (i // per, 0, 0))
    args = [x, y, g_post.reshape(1, D), gate]
    in_specs = [row, row, vec, mod]
    out_shape = [jax.ShapeDtypeStruct((M, D), F32)]
    out_specs = [row]
    if nxt is not None:
        gain, shift, scale = nxt
        args += [gain.reshape(1, D), shift, scale]
        in_specs += [vec, mod, mod]
        out_shape.append(jax.ShapeDtypeStruct((M, D), BF16))
        out_specs.append(row)
    out = pl.pallas_call(
        functools.partial(_post_body, with_next=nxt is not None),
        grid=(M // tr,),
        in_specs=in_specs,
        out_specs=out_specs,
        out_shape=out_shape,
        input_output_aliases={0: 0} if x.shape == y.shape else {},
        compiler_params=_params(("parallel",)),
        name="residual_norm",
    )(*args)
    return (out[0], out[1]) if nxt is not None else (out[0], None)


def _readout_body(o_ref, g_ref, gain_ref, h_ref):
    g = g_ref[...]
    h_ref[...] = (_rms(o_ref[...], gain_ref[...]) * (g * _sigmoid(g))).astype(h_ref.dtype)


def hgrn2_readout_input(o, proj, gain, rows):
    M, D = rows, o.shape[1]
    tr = _tile(M, 256, 8)
    row = pl.BlockSpec((tr, D), lambda i: (i, 0))
    return pl.pallas_call(
        _readout_body,
        grid=(M // tr,),
        in_specs=[row, pl.BlockSpec((None, tr, D), lambda i: (4, i, 0)),
                  pl.BlockSpec((1, D), lambda i: (0, 0))],
        out_specs=row,
        out_shape=jax.ShapeDtypeStruct((M, D), BF16),
        compiler_params=_params(("parallel",)),
        name="hgrn2_readout_input",
    )(o, proj, gain.reshape(1, D))


def _latent_norm_body(c_ref, qg_ref, kvg_ref, cq_ref, ckv_ref):
    c = c_ref[...]
    cq_ref[...] = _rms(c[:, :MLA_Q_RANK], qg_ref[...]).astype(cq_ref.dtype)
    ckv_ref[...] = _rms(c[:, MLA_Q_RANK:], kvg_ref[...]).astype(ckv_ref.dtype)


def mla_latent_norms(c, q_gain, kv_gain):
    M, W = c.shape
    tr = _tile(M, 512, 8)
    return pl.pallas_call(
        _latent_norm_body,
        grid=(M // tr,),
        in_specs=[pl.BlockSpec((tr, W), lambda i: (i, 0)),
                  pl.BlockSpec((1, MLA_Q_RANK), lambda i: (0, 0)),
                  pl.BlockSpec((1, MLA_KV_RANK), lambda i: (0, 0))],
        out_specs=[pl.BlockSpec((tr, MLA_Q_RANK), lambda i: (i, 0)),
                   pl.BlockSpec((tr, MLA_KV_RANK), lambda i: (i, 0))],
        out_shape=[jax.ShapeDtypeStruct((M, MLA_Q_RANK), BF16),
                   jax.ShapeDtypeStruct((M, MLA_KV_RANK), BF16)],
        compiler_params=_params(("parallel",)),
        name="mla_latent_norms",
    )(c, q_gain.reshape(1, -1), kv_gain.reshape(1, -1))


def _hg_proj_epilogue(acc, o_ref, lb_ref, omlb_ref, *, tiles_per_section):
    sec = pl.program_id(1) // tiles_per_section

    @pl.when(sec == 0)
    def _():
        o_ref[...] = acc * _sigmoid(acc)

    @pl.when(jnp.logical_or(sec == 1, sec == 4))
    def _():
        o_ref[...] = acc

    @pl.when(jnp.logical_or(sec == 2, sec == 3))
    def _():
        t = jnp.exp(-jnp.abs(acc))
        sig = jnp.where(acc >= 0.0, 1.0, t) / (1.0 + t)
        one_m_lb = omlb_ref[...]
        u = lb_ref[...] + one_m_lb * sig
        o_ref[...] = jnp.where(u > 0.0, jnp.log(u), acc + jnp.log(one_m_lb))


def hgrn2_project(h, w_in, lb, tn=512):
    M, D = h.shape
    tm, tn = _tile(M, 1024, 8), _tile(D, tn)
    per = D // tn
    zeros = jnp.zeros((1, D), F32)
    aux_a = jnp.concatenate([zeros, zeros, lb, zeros], axis=0).reshape(5, 1, D)
    aux_b = jnp.concatenate([zeros, zeros, 1.0 - lb, zeros], axis=0).reshape(5, 1, D)
    aux_spec = pl.BlockSpec((None, 1, tn), lambda i, j, k: (j // per, 0, j % per))
    return matmul(
        h, w_in, out_dtype=F32,
        epilogue=functools.partial(_hg_proj_epilogue, tiles_per_section=per),
        aux=((aux_a, aux_spec), (aux_b, aux_spec)),
        out_shape=jax.ShapeDtypeStruct((5, M, D), F32),
        out_spec=pl.BlockSpec((None, tm, tn), lambda i, j, k: (j // per, i, j % per)),
        tm=tm, tn=tn, name="hgrn2_project")


def _gla_tables(reverse):
    C, L = GLA_CHUNK, GLA_LEVELS
    r = np.arange(C)
    mats = []
    if not reverse:
        mats.append((r[None, :] <= r[:, None]))
        mats.append((r[None, :] > r[:, None]))
    else:
        mats.append((r[None, :] >= r[:, None]))
        mats.append((r[None, :] < r[:, None]))
    lvl = np.full((C, C), -1, np.int32)
    lvl[r, r] = L
    for l in range(L):
        w = C >> (l + 1)
        blk = r // w
        odd = (blk % 2) == 1
        rho = ((blk // 2) * 2 + 1) * w
        m = np.zeros((C, C), bool)
        for t in range(C):
            if not reverse:
                cols = (r >= rho[t]) & (r <= t) if odd[t] else (r > t) & (r < rho[t])
            else:
                cols = (r >= rho[t]) & (r < t) if odd[t] else (r >= t) & (r < rho[t])
            m[t] = cols
        mats.append(m)
        same_parent = (r[:, None] // (2 * w)) == (r[None, :] // (2 * w))
        if not reverse:
            pair = same_parent & odd[:, None] & ~odd[None, :]
        else:
            pair = same_parent & ~odd[:, None] & odd[None, :]
        lvl[pair] = l
    tail = np.zeros((16, C), bool)
    tail[0] = True
    mats.append(tail)
    m = np.concatenate(mats, axis=0).astype(np.float32)
    return np.concatenate([m, m, m], axis=1), lvl


def _gla_body(q_ref, v_ref, g_ref, mcat_ref, lvl_ref, *rest, heads, accumulate):
    if accumulate:
        oprev_ref, o_ref, st_ref, ex_ref = rest
    else:
        o_ref, st_ref, ex_ref = rest
    C, L = GLA_CHUNK, GLA_LEVELS

    @pl.when(pl.program_id(2) == 0)
    def _():
        st_ref[...] = jnp.zeros_like(st_ref)

    g = g_ref[...]
    ex_ref[...] = jnp.dot(mcat_ref[...], jnp.concatenate(_split3(g), axis=0),
                          preferred_element_type=F32)
    lvl = lvl_ref[...]
    masks = [lvl == l for l in range(L + 1)]
    hs = [slice(h * HEAD_DIM, (h + 1) * HEAD_DIM) for h in range(heads)]
    dot_nt = lambda a, b: lax.dot_general(a, b, _NT, preferred_element_type=F32)

    q = [q_ref[:, sl] for sl in hs]
    k = [1.0 - jnp.exp(g[:, sl]) for sl in hs]
    st = [st_ref[h] for h in range(heads)]
    o = [dot_nt((q[h] * jnp.exp(ex_ref[0:C, hs[h]])).astype(BF16), st[h].astype(BF16))
         for h in range(heads)]
    att = [dot_nt(q[h].astype(BF16), k[h].astype(BF16)) for h in range(heads)]
    att = [jnp.where(masks[L], a, 0.0) for a in att]
    for l in range(L):
        e = [jnp.exp(ex_ref[(2 + l) * C:(3 + l) * C, sl]) for sl in hs]
        p = [dot_nt((q[h] * e[h]).astype(BF16), (k[h] * e[h]).astype(BF16)) for h in range(heads)]
        att = [jnp.where(masks[l], p[h], att[h]) for h in range(heads)]
    v = [v_ref[:, sl] for sl in hs]
    for h in range(heads):
        oh = o[h] + jnp.dot(att[h].astype(BF16), v[h].astype(BF16), preferred_element_type=F32)
        if accumulate:
            oh = oh + oprev_ref[:, hs[h]]
        o_ref[:, hs[h]] = oh
    for h in range(heads):
        ke = (k[h] * jnp.exp(ex_ref[C:2 * C, hs[h]])).astype(BF16)
        decay = jnp.exp(ex_ref[(2 + L) * C:(2 + L) * C + 1, hs[h]])
        st_ref[h] = decay * st[h] + jnp.dot(v[h].T.astype(BF16), ke, preferred_element_type=F32)


def gla_scan(proj, *, batch, seq, ctx_len, reverse, o_prev=None):
    _, M, D = proj.shape
    C = GLA_CHUNK
    heads = min(GLA_HEADS_PER_STEP, D // HEAD_DIM)
    width = heads * HEAD_DIM
    n_ctx, n_lat = ctx_len // C, seq // C
    ctx_base = batch * seq // C
    mcat, lvl = _gla_tables(reverse)
    mcat = jnp.asarray(mcat, BF16)
    lvl = jnp.asarray(lvl)
    g_section = 3 if reverse else 2

    def row_block(b, n):
        if reverse:
            return jnp.where(n < n_ctx, ctx_base + b * n_ctx + (n_ctx - 1 - n),
                             b * n_lat + (n_lat - 1 - (n - n_ctx)))
        return jnp.where(n < n_ctx, ctx_base + b * n_ctx + n, b * n_lat + (n - n_ctx))

    def section(s):
        return pl.BlockSpec((None, C, width), lambda b, hg, n: (s, row_block(b, n), hg))

    row = pl.BlockSpec((C, width), lambda b, hg, n: (row_block(b, n), hg))
    const = lambda shape: pl.BlockSpec(shape, lambda b, hg, n: (0, 0))
    args = [proj, proj, proj, mcat, lvl]
    in_specs = [section(0), section(1), section(g_section), const(mcat.shape), const(lvl.shape)]
    aliases = {}
    if o_prev is not None:
        args.append(o_prev)
        in_specs.append(row)
        aliases = {5: 0}
    return pl.pallas_call(
        functools.partial(_gla_body, heads=heads, accumulate=o_prev is not None),
        grid=(batch, D // width, n_ctx + n_lat),
        in_specs=in_specs,
        out_specs=row,
        out_shape=jax.ShapeDtypeStruct((M, D), F32),
        scratch_shapes=[pltpu.VMEM((heads, HEAD_DIM, HEAD_DIM), F32),
                        pltpu.VMEM(mcat.shape[:1] + (width,), F32)],
        input_output_aliases=aliases,
        compiler_params=_params(("parallel", "parallel", "arbitrary")),
        name="gla_bwd" if reverse else "gla_fwd",
    )(*args)


def _rope_epilogue(acc, o_ref, cos_ref, sin_ref):
    cos, sin = cos_ref[...], sin_ref[...]
    lane = lax.broadcasted_iota(jnp.int32, cos.shape, 1)
    first_half = (lane % 32) < 16
    for hh in range(acc.shape[1] // V7X_LANES):
        sl = slice(hh * V7X_LANES, (hh + 1) * V7X_LANES)
        x = acc[:, sl]
        rot = jnp.where(first_half, -pltpu.roll(x, V7X_LANES - 16, 1), pltpu.roll(x, 16, 1))
        o_ref[:, sl] = (x * cos + rot * sin).astype(o_ref.dtype)


def rope_matmul(a, w, cos, sin, name):
    M = a.shape[0]
    tm = _tile(M, 1024, 8)
    tab = pl.BlockSpec((tm, V7X_LANES), lambda i, j, k: (i, 0))
    return matmul(a, w, out_dtype=BF16, epilogue=_rope_epilogue,
                  aux=((cos, tab), (sin, tab)), tm=tm, name=name)


def _attn_latent_body(qn_ref, qr_ref, knc_ref, krc_ref, vc_ref, knl_ref, krl_ref, vl_ref, o_ref,
                      kcat_ref, vaug_ref, *, ctx_len, chunk):
    @pl.when(pl.program_id(2) == 0)
    def _():
        kcat_ref[0:ctx_len, 0:HEAD_DIM] = knc_ref[...]
        kcat_ref[0:ctx_len, HEAD_DIM:] = krc_ref[...]
        kcat_ref[ctx_len:, 0:HEAD_DIM] = knl_ref[...]
        kcat_ref[ctx_len:, HEAD_DIM:] = krl_ref[...]
        lane = lax.broadcasted_iota(jnp.int32, (vaug_ref.shape[0], HEAD_DIM), 1)
        vaug_ref[:, HEAD_DIM:] = jnp.where(lane == 0, 1.0, 0.0).astype(BF16)
        vaug_ref[0:ctx_len, 0:HEAD_DIM] = vc_ref[...]
        vaug_ref[ctx_len:, 0:HEAD_DIM] = vl_ref[...]

    q = jnp.concatenate([qn_ref[...], qr_ref[...]], axis=1)
    total = kcat_ref.shape[0]
    spans = [(0, ctx_len)] + [(c, chunk) for c in range(ctx_len, total, chunk)]
    m = acc = None
    for start, size in spans:
        s = lax.dot_general(q, kcat_ref[start:start + size, :], _NT, preferred_element_type=F32)
        m_c = jnp.max(s, axis=-1, keepdims=True)
        m_new = m_c if m is None else jnp.maximum(m, m_c)
        p = jnp.exp2(s - m_new).astype(BF16)
        pv = jnp.dot(p, vaug_ref[start:start + size, :], preferred_element_type=F32)
        acc = pv if m is None else jnp.exp2(m - m_new) * acc + pv
        m = m_new
    o_ref[...] = (acc[:, :HEAD_DIM] / acc[:, HEAD_DIM:HEAD_DIM + 1]).astype(o_ref.dtype)


def _attn_context_body(qn_ref, qr_ref, kn_ref, kr_ref, v_ref, alias_ref, o_ref):
    del alias_ref
    s = (lax.dot_general(qn_ref[...], kn_ref[...], _NT, preferred_element_type=F32)
         + lax.dot_general(qr_ref[...], kr_ref[...], _NT, preferred_element_type=F32))
    p = jnp.exp2(s - jnp.max(s, axis=-1, keepdims=True))
    o = jnp.dot(p.astype(BF16), v_ref[...], preferred_element_type=F32)
    o_ref[...] = (o / jnp.sum(p, axis=-1, keepdims=True)).astype(o_ref.dtype)


def mla_attention(qn, qr, kn, kr, v, *, batch, seq, ctx_len, context_queries=True):
    M, D = qn.shape
    H = D // HEAD_DIM
    tq = _tile(seq, ATTN_Q_TILE, 8)
    nq = seq // tq
    chunk = _tile(seq, ATTN_KEY_CHUNK, 8)
    ctx_base = batch * seq // ctx_len
    blk = lambda rows, fn: pl.BlockSpec((rows, HEAD_DIM), fn)
    ctx_keys = [blk(ctx_len, lambda b, h, i: (ctx_base + b, h)),
                blk(ctx_len, lambda b, h, i: (ctx_base + b, 0)),
                blk(ctx_len, lambda b, h, i: (ctx_base + b, h))]
    lat_keys = [blk(seq, lambda b, h, i: (b, h)),
                blk(seq, lambda b, h, i: (b, 0)),
                blk(seq, lambda b, h, i: (b, h))]
    q_lat = blk(tq, lambda b, h, i: (b * nq + i, h))
    o_lat = pl.pallas_call(
        functools.partial(_attn_latent_body, ctx_len=ctx_len, chunk=chunk),
        grid=(batch, H, nq),
        in_specs=[q_lat, q_lat] + ctx_keys + lat_keys,
        out_specs=q_lat,
        out_shape=jax.ShapeDtypeStruct((M, D), BF16),
        scratch_shapes=[pltpu.VMEM((ctx_len + seq, 2 * HEAD_DIM), BF16),
                        pltpu.VMEM((ctx_len + seq, 2 * HEAD_DIM), BF16)],
        compiler_params=_params(("parallel", "parallel", "arbitrary")),
        name="mla_attention_latent",
    )(qn, qr, kn, kr, v, kn, kr, v)
    if not context_queries:
        return o_lat
    q_ctx = blk(ctx_len, lambda b, h, i: (ctx_base + b, h))
    return pl.pallas_call(
        _attn_context_body,
        grid=(batch, H, 1),
        in_specs=[q_ctx, q_ctx] + ctx_keys + [pl.BlockSpec(memory_space=pl.ANY)],
        out_specs=q_ctx,
        out_shape=jax.ShapeDtypeStruct((M, D), BF16),
        input_output_aliases={5: 0},
        compiler_params=_params(("parallel", "parallel", "arbitrary")),
        name="mla_attention_context",
    )(qn, qr, kn, kr, v, o_lat)


def _rope_tables(batch, seq, ctx_len):
    t = jnp.arange(seq)
    row = (t // GRID_W).astype(F32)
    col = (t % GRID_W).astype(F32)
    axis_dim = MLA_ROPE // 2
    inv_freq = ROPE_THETA ** (-jnp.arange(0, axis_dim, 2, dtype=F32) / axis_dim)
    ang_r = row[:, None] * inv_freq
    ang_c = col[:, None] * inv_freq
    ang = jnp.concatenate([ang_r, ang_r, ang_c, ang_c], axis=-1)
    pad = V7X_LANES - MLA_ROPE
    cos = jnp.pad(jnp.cos(ang), ((0, 0), (0, pad)), constant_values=1.0)
    sin = jnp.pad(jnp.sin(ang), ((0, 0), (0, pad)))
    n_ctx = batch * ctx_len
    cos = jnp.concatenate([jnp.tile(cos, (batch, 1)), jnp.ones((n_ctx, V7X_LANES), F32)], axis=0)
    sin = jnp.concatenate([jnp.tile(sin, (batch, 1)), jnp.zeros((n_ctx, V7X_LANES), F32)], axis=0)
    return cos, sin


def _pad_heads(w, heads, width):
    K = w.shape[0]
    w = w.reshape(K, heads, width)
    w = jnp.pad(w, ((0, 0), (0, 0), (0, V7X_LANES - width)))
    return w.reshape(K, heads * V7X_LANES).astype(BF16)


def kernel(x, c, ctx, c_ctx, ada_down, ada_up, ada_bias, norm_mix_pre, norm_mix_post, norm_mlp_pre,
           norm_mlp_post, mlp_w1, mlp_w2, hg_w_in, hg_lb_logits, hg_norm, hg_w_out, mla_w_down,
           mla_q_norm, mla_w_uq, mla_kv_norm, mla_w_ukv, mla_w_out):
    B, T, D = x.shape
    Lc = ctx.shape[1]
    depth = ada_down.shape[0]
    H = D // HEAD_DIM
    n_lat, n_ctx = B * T, B * Lc
    M = n_lat + n_ctx
    G = math.gcd(T, n_ctx)
    assert T % GLA_CHUNK == 0 and Lc % GLA_CHUNK == 0 and T % Lc == 0 and D % HEAD_DIM == 0

    cond = jnp.concatenate([c, c_ctx[None, :], jnp.zeros((7 - B % 8, D), F32)], axis=0)
    mods = ada_modulations(cond, ada_down, ada_up, ada_bias)
    group_src = np.concatenate([np.repeat(np.arange(B), T // G), np.full(n_ctx // G, B)])
    mods = mods[:, group_src, :].reshape(depth, M // G, 1, N_MOD, D)
    mod = lambda layer, idx: mods[layer, :, :, idx, :]

    lb = jnp.cumsum(jax.nn.softmax(hg_lb_logits.astype(F32), axis=0), axis=0)
    lb = lb - lb[:1]

    cos, sin = _rope_tables(B, T, Lc)
    xs = jnp.concatenate([x.reshape(n_lat, D), ctx.reshape(n_ctx, D)], axis=0)
    h = norm_modulate(xs, norm_mix_pre[0], mod(0, 0), mod(0, 1), G)

    for layer in range(depth):
        j = layer // 2
        rows = n_lat if layer == depth - 1 else M
        if layer % 2 == 0:
            proj = hgrn2_project(h, hg_w_in[j].astype(BF16), lb[j])
            o = gla_scan(proj, batch=B, seq=T, ctx_len=Lc, reverse=False)
            o = gla_scan(proj, batch=B, seq=T, ctx_len=Lc, reverse=True, o_prev=o)
            r = hgrn2_readout_input(o, proj, hg_norm[j], rows)
            y = matmul(r, hg_w_out[j].astype(BF16), out_dtype=F32, name="hgrn2_out")
        else:
            w_down = mla_w_down[j]
            n_lat_rank = MLA_Q_RANK + MLA_KV_RANK
            lat = matmul(h, w_down[:, :n_lat_rank].astype(BF16), out_dtype=F32, tn=512,
                         name="mla_down")
            kr = rope_matmul(h, _pad_heads(w_down[:, n_lat_rank:], 1, MLA_ROPE), cos, sin,
                             "mla_rope_key")
            cq, ckv = mla_latent_norms(lat, mla_q_norm[j], mla_kv_norm[j])
            w_uq = (mla_w_uq[j] * ATTN_EXP2_SCALE).reshape(MLA_Q_RANK, H, HEAD_DIM + MLA_ROPE)
            qn = matmul(cq, w_uq[:, :, :HEAD_DIM].reshape(MLA_Q_RANK, D).astype(BF16),
                        out_dtype=BF16, name="mla_q_nope")
            qr = rope_matmul(cq, _pad_heads(w_uq[:, :, HEAD_DIM:].reshape(MLA_Q_RANK, -1), H,
                                            MLA_ROPE), cos, sin, "mla_q_rope")
            w_ukv = mla_w_ukv[j].reshape(MLA_KV_RANK, H, 2 * HEAD_DIM)
            kn = matmul(ckv, w_ukv[:, :, :HEAD_DIM].reshape(MLA_KV_RANK, D).astype(BF16),
                        out_dtype=BF16, name="mla_k_nope")
            v = matmul(ckv, w_ukv[:, :, HEAD_DIM:].reshape(MLA_KV_RANK, D).astype(BF16),
                       out_dtype=BF16, name="mla_value")
            o = mla_attention(qn, qr, kn, kr, v, batch=B, seq=T, ctx_len=Lc,
                              context_queries=rows == M)
            y = matmul(o, mla_w_out[j].astype(BF16), out_dtype=F32, rows=rows, name="mla_out")
        xs, h = residual_norm(xs, y, norm_mix_post[layer], mod(layer, 2), G,
                              nxt=(norm_mlp_pre[layer], mod(layer, 3), mod(layer, 4)))
        hid = matmul(h, mlp_w1[layer].astype(BF16), out_dtype=BF16, epilogue=_epi_relu2,
                     name="mlp_up")
        m = matmul(hid, mlp_w2[layer].astype(BF16), out_dtype=F32, name="mlp_down")
        nxt = None
        if layer + 1 < depth:
            nxt = (norm_mix_pre[layer + 1], mod(layer + 1, 0), mod(layer + 1, 1))
        xs, h = residual_norm(xs, m, norm_mlp_post[layer], mod(layer, 5), G, nxt=nxt)
    return xs.reshape(B, T, D)
```

```python
import functools
import math

import numpy as np
import jax
import jax.numpy as jnp
from jax import lax
from jax.experimental import pallas as pl
from jax.experimental.pallas import tpu as pltpu

F32 = jnp.float32
BF16 = jnp.bfloat16

NORM_EPS = 1e-6
N_MOD = 6
GRID_W = 64
HEAD_DIM = 128
MLA_ROPE = 64
MLA_Q_RANK = 1024
MLA_KV_RANK = 512
MLA_SCALE = (HEAD_DIM + MLA_ROPE) ** -0.5
ATTN_EXP2_SCALE = MLA_SCALE * math.log2(math.e)
ATTN_Q_TILE = 2048
ATTN_KEY_CHUNK = 1024
ROPE_THETA = 10000.0
GLA_CHUNK = 64
GLA_HEADS_PER_STEP = 32
GLA_LEVELS = 6

V7X_LANES = 128
V7X_VMEM_LIMIT = 56 * 1024 * 1024

MM_FULL_K_MAX = 4096
MM_SPLIT_K = 4096

_NT = (((1,), (1,)), ((), ()))


def _tile(dim, pref, align=V7X_LANES):
    if dim <= pref:
        return dim
    t = (pref // align) * align
    while t >= align:
        if dim % t == 0:
            return t
        t -= align
    return dim


def _params(sem):
    return pltpu.CompilerParams(dimension_semantics=sem, vmem_limit_bytes=V7X_VMEM_LIMIT)


def _sigmoid(x):
    return 1.0 / (1.0 + jnp.exp(-x))


def _split3(x):
    hi = x.astype(BF16)
    r = x - hi.astype(F32)
    mid = r.astype(BF16)
    lo = (r - mid.astype(F32)).astype(BF16)
    return hi, mid, lo


def _mm_body(a_ref, b_ref, *rest, nk, n_aux, epilogue):
    aux = rest[:n_aux]
    o_ref = rest[n_aux]
    if nk == 1:
        epilogue(jnp.dot(a_ref[...], b_ref[...], preferred_element_type=F32), o_ref, *aux)
        return
    acc_ref = rest[n_aux + 1]
    k = pl.program_id(2)

    @pl.when(k == 0)
    def _():
        acc_ref[...] = jnp.dot(a_ref[...], b_ref[...], preferred_element_type=F32)

    @pl.when(k > 0)
    def _():
        acc_ref[...] += jnp.dot(a_ref[...], b_ref[...], preferred_element_type=F32)

    @pl.when(k == nk - 1)
    def _():
        epilogue(acc_ref[...], o_ref, *aux)


def _epi_plain(acc, o_ref):
    o_ref[...] = acc.astype(o_ref.dtype)


def _epi_relu2(acc, o_ref):
    r = jnp.maximum(acc, 0.0)
    o_ref[...] = (r * r).astype(o_ref.dtype)


def matmul(a, b, *, out_dtype, epilogue=_epi_plain, aux=(), out_shape=None, out_spec=None,
           tm=1024, tn=1024, tk=None, rows=None, name="mm"):
    K = a.shape[1]
    M = a.shape[0] if rows is None else rows
    _, N = b.shape
    if tk is None:
        tk = K if K <= MM_FULL_K_MAX else MM_SPLIT_K
    tm, tn, tk = _tile(M, tm, 8), _tile(N, tn), _tile(K, tk)
    nk = K // tk
    if out_shape is None:
        out_shape = jax.ShapeDtypeStruct((M, N), out_dtype)
        out_spec = pl.BlockSpec((tm, tn), lambda i, j, k: (i, j))
    in_specs = [pl.BlockSpec((tm, tk), lambda i, j, k: (i, k)),
                pl.BlockSpec((tk, tn), lambda i, j, k: (k, j))]
    in_specs += [spec for _, spec in aux]
    scratch = [] if nk == 1 else [pltpu.VMEM((tm, tn), F32)]
    return pl.pallas_call(
        functools.partial(_mm_body, nk=nk, n_aux=len(aux), epilogue=epilogue),
        grid=(M // tm, N // tn, nk),
        in_specs=in_specs,
        out_specs=out_spec,
        out_shape=out_shape,
        scratch_shapes=scratch,
        compiler_params=_params(("parallel", "parallel", "arbitrary")),
        name=name,
    )(a, b, *[arr for arr, _ in aux])


def _dot3(a, b):
    a0, a1, a2 = _split3(a)
    b0, b1, b2 = _split3(b)
    d = lambda x, y: jnp.dot(x, y, preferred_element_type=F32)
    return ((d(a0, b0) + (d(a0, b1) + d(a1, b0)))
            + ((d(a0, b2) + d(a2, b0)) + d(a1, b1)))


def _ada_body(cond_ref, down_ref, up_ref, bias_ref, o_ref):
    cond = cond_ref[...]
    s = cond * _sigmoid(cond)
    t = _dot3(s, down_ref[...])
    o_ref[...] = _dot3(t, up_ref[...]) + bias_ref[...]


def ada_modulations(cond, ada_down, ada_up, ada_bias):
    depth, D, rank = ada_down.shape
    R = cond.shape[0]
    N = ada_up.shape[2]
    tn = _tile(N, 2048)
    return pl.pallas_call(
        _ada_body,
        grid=(depth, N // tn),
        in_specs=[pl.BlockSpec((R, D), lambda l, j: (0, 0)),
                  pl.BlockSpec((None, D, rank), lambda l, j: (l, 0, 0)),
                  pl.BlockSpec((None, rank, tn), lambda l, j: (l, 0, j)),
                  pl.BlockSpec((None, 1, tn), lambda l, j: (l, 0, j))],
        out_specs=pl.BlockSpec((None, R, tn), lambda l, j: (l, 0, j)),
        out_shape=jax.ShapeDtypeStruct((depth, R, N), F32),
        compiler_params=_params(("parallel", "arbitrary")),
        name="ada_mod",
    )(cond, ada_down, ada_up, ada_bias.reshape(depth, 1, N))


def _rms(x, gain):
    ms = jnp.mean(x * x, axis=-1, keepdims=True)
    return x * lax.rsqrt(ms + NORM_EPS) * gain


def _pre_body(x_ref, g_ref, sh_ref, sc_ref, h_ref):
    h = _rms(x_ref[...], g_ref[...])
    h_ref[...] = (h * (1.0 + sc_ref[...]) + sh_ref[...]).astype(h_ref.dtype)


def norm_modulate(x, gain, shift, scale, group):
    M, D = x.shape
    tr = _tile(group, 256, 8)
    per = group // tr
    row = pl.BlockSpec((tr, D), lambda i: (i, 0))
    vec = pl.BlockSpec((1, D), lambda i: (0, 0))
    mod = pl.BlockSpec((None, 1, D), lambda i: (i // per, 0, 0))
    return pl.pallas_call(
        _pre_body,
        grid=(M // tr,),
        in_specs=[row, vec, mod, mod],
        out_specs=row,
        out_shape=jax.ShapeDtypeStruct((M, D), BF16),
        compiler_params=_params(("parallel",)),
        name="norm_modulate",
    )(x, gain.reshape(1, D), shift, scale)


def _post_body(x_ref, y_ref, gpost_ref, gate_ref, *rest, with_next):
    x = x_ref[...] + gate_ref[...] * _rms(y_ref[...], gpost_ref[...])
    if with_next:
        gpre_ref, sh_ref, sc_ref, xo_ref, h_ref = rest
        xo_ref[...] = x
        h = _rms(x, gpre_ref[...])
        h_ref[...] = (h * (1.0 + sc_ref[...]) + sh_ref[...]).astype(h_ref.dtype)
    else:
        (xo_ref,) = rest
        xo_ref[...] = x


def residual_norm(x, y, g_post, gate, group, nxt=None):
    M, D = y.shape
    tr = _tile(group, 256, 8)
    per = group // tr
    row = pl.BlockSpec((tr, D), lambda i: (i, 0))
    vec = pl.BlockSpec((1, D), lambda i: (0, 0))
    mod = pl.BlockSpec((None, 1, D), lambda i: (i // per, 0, 0))
    args = [x, y, g_post.reshape(1, D), gate]
    in_specs = [row, row, vec, mod]
    out_shape = [jax.ShapeDtypeStruct((M, D), F32)]
    out_specs = [row]
    if nxt is not None:
        gain, shift, scale = nxt
        args += [gain.reshape(1, D), shift, scale]
        in_specs += [vec, mod, mod]
        out_shape.append(jax.ShapeDtypeStruct((M, D), BF16))
        out_specs.append(row)
    out = pl.pallas_call(
        functools.partial(_post_body, with_next=nxt is not None),
        grid=(M // tr,),
        in_specs=in_specs,
        out_specs=out_specs,
        out_shape=out_shape,
        input_output_aliases={0: 0} if x.shape == y.shape else {},
        compiler_params=_params(("parallel",)),
        name="residual_norm",
    )(*args)
    return (out[0], out[1]) if nxt is not None else (out[0], None)


def _latent_norm_body(c_ref, qg_ref, kvg_ref, cq_ref, ckv_ref):
    c = c_ref[...]
    cq_ref[...] = _rms(c[:, :MLA_Q_RANK], qg_ref[...]).astype(cq_ref.dtype)
    ckv_ref[...] = _rms(c[:, MLA_Q_RANK:], kvg_ref[...]).astype(ckv_ref.dtype)


def mla_latent_norms(c, q_gain, kv_gain):
    M, W = c.shape
    tr = _tile(M, 512, 8)
    return pl.pallas_call(
        _latent_norm_body,
        grid=(M // tr,),
        in_specs=[pl.BlockSpec((tr, W), lambda i: (i, 0)),
                  pl.BlockSpec((1, MLA_Q_RANK), lambda i: (0, 0)),
                  pl.BlockSpec((1, MLA_KV_RANK), lambda i: (0, 0))],
        out_specs=[pl.BlockSpec((tr, MLA_Q_RANK), lambda i: (i, 0)),
                   pl.BlockSpec((tr, MLA_KV_RANK), lambda i: (i, 0))],
        out_shape=[jax.ShapeDtypeStruct((M, MLA_Q_RANK), BF16),
                   jax.ShapeDtypeStruct((M, MLA_KV_RANK), BF16)],
        compiler_params=_params(("parallel",)),
        name="mla_latent_norms",
    )(c, q_gain.reshape(1, -1), kv_gain.reshape(1, -1))


def _hg_proj_epilogue(acc, o_ref, lb_ref, omlb_ref, *, tiles_per_section):
    sec = pl.program_id(1) // tiles_per_section

    @pl.when(sec == 0)
    def _():
        o_ref[...] = acc * _sigmoid(acc)

    @pl.when(jnp.logical_or(sec == 1, sec == 4))
    def _():
        o_ref[...] = acc

    @pl.when(jnp.logical_or(sec == 2, sec == 3))
    def _():
        t = jnp.exp(-jnp.abs(acc))
        sig = jnp.where(acc >= 0.0, 1.0, t) / (1.0 + t)
        one_m_lb = omlb_ref[...]
        u = lb_ref[...] + one_m_lb * sig
        o_ref[...] = jnp.where(u > 0.0, jnp.log(u), acc + jnp.log(one_m_lb))


def hgrn2_project(h, w_in, lb, tn=512):
    M, D = h.shape
    tm, tn = _tile(M, 1024, 8), _tile(D, tn)
    per = D // tn
    zeros = jnp.zeros((1, D), F32)
    aux_a = jnp.concatenate([zeros, zeros, lb, zeros], axis=0).reshape(5, 1, D)
    aux_b = jnp.concatenate([zeros, zeros, 1.0 - lb, zeros], axis=0).reshape(5, 1, D)
    aux_spec = pl.BlockSpec((None, 1, tn), lambda i, j, k: (j // per, 0, j % per))
    return matmul(
        h, w_in, out_dtype=F32,
        epilogue=functools.partial(_hg_proj_epilogue, tiles_per_section=per),
        aux=((aux_a, aux_spec), (aux_b, aux_spec)),
        out_shape=jax.ShapeDtypeStruct((5, M, D), F32),
        out_spec=pl.BlockSpec((None, tm, tn), lambda i, j, k: (j // per, i, j % per)),
        tm=tm, tn=tn, name="hgrn2_project")


def _gla_tables(reverse):
    C, L = GLA_CHUNK, GLA_LEVELS
    r = np.arange(C)
    mats = [(r[None, :] >= r[:, None]) if reverse else (r[None, :] <= r[:, None])]
    lvl = np.full((C, C), -1, np.int32)
    lvl[r, r] = L
    for l in range(L):
        w = C >> (l + 1)
        blk = r // w
        odd = (blk % 2) == 1
        rho = ((blk // 2) * 2 + 1) * w
        m = np.zeros((C, C), bool)
        for t in range(C):
            if not reverse:
                cols = (r >= rho[t]) & (r <= t) if odd[t] else (r > t) & (r < rho[t])
            else:
                cols = (r >= rho[t]) & (r < t) if odd[t] else (r >= t) & (r < rho[t])
            m[t] = cols
        if 2 <= w < 8:
            mats.append(m)
        same_parent = (r[:, None] // (2 * w)) == (r[None, :] // (2 * w))
        if not reverse:
            pair = same_parent & odd[:, None] & ~odd[None, :]
        else:
            pair = same_parent & ~odd[:, None] & odd[None, :]
        lvl[pair] = l
    tail = np.zeros((16, C), bool)
    tail[0] = True
    mats.append(tail)
    m = np.concatenate(mats, axis=0).astype(np.float32)
    return np.concatenate([m, m, m], axis=1), lvl


def _gla_body(q_ref, v_ref, g_ref, mcat_ref, lvl_ref, oprev_ref, *rest, heads, reverse, readout):
    if readout:
        gate_ref, gain_ref, o_ref, st_ref, ex_ref, osum_ref = rest
    else:
        o_ref, st_ref, ex_ref = rest
        osum_ref = o_ref
    C, L = GLA_CHUNK, GLA_LEVELS

    @pl.when(pl.program_id(2) == 0)
    def _():
        st_ref[...] = jnp.zeros_like(st_ref)

    g = g_ref[...]
    ex_ref[...] = jnp.dot(mcat_ref[...], jnp.concatenate(_split3(g), axis=0),
                          preferred_element_type=F32)
    lvl = lvl_ref[...]
    masks = [lvl == l for l in range(L + 1)]
    row = lax.broadcasted_iota(jnp.int32, (C, HEAD_DIM), 0)
    pair_query = (row % 2) == (0 if reverse else 1)
    hs = [slice(h * HEAD_DIM, (h + 1) * HEAD_DIM) for h in range(heads)]
    dot_nt = lambda a, b: lax.dot_general(a, b, _NT, preferred_element_type=F32)

    q = [q_ref[:, sl] for sl in hs]
    f = [jnp.exp(g[:, sl]) for sl in hs]
    k = [1.0 - fh for fh in f]
    qb = [x.astype(BF16) for x in q]
    kb = [x.astype(BF16) for x in k]
    st = [st_ref[h] for h in range(heads)]
    o = [dot_nt((q[h] * jnp.exp(ex_ref[0:C, hs[h]])).astype(BF16), st[h].astype(BF16))
         for h in range(heads)]
    att = [jnp.where(masks[L], dot_nt(qb[h], kb[h]), 0.0) for h in range(heads)]
    mxu_level = 0
    for l in range(L):
        w = C >> (l + 1)
        if w >= 8:
            src = [p + (w if reverse else w - 1) for p in range(0, C, 2 * w)]
            e = []
            for sl in hs:
                bound = jnp.concatenate(
                    [jnp.broadcast_to(ex_ref[r:r + 1, sl], (2 * w, HEAD_DIM)) for r in src], axis=0)
                e.append(jnp.exp(-jnp.abs(ex_ref[0:C, sl] - bound)).astype(BF16))
        elif w >= 2:
            lo = (1 + mxu_level) * C
            mxu_level += 1
            e = [jnp.exp(ex_ref[lo:lo + C, sl]).astype(BF16) for sl in hs]
        else:
            e = [jnp.where(pair_query, fh, 1.0).astype(BF16) for fh in f]
        p = [dot_nt(qb[h] * e[h], kb[h] * e[h]) for h in range(heads)]
        att = [jnp.where(masks[l], p[h], att[h]) for h in range(heads)]
    v = [v_ref[:, sl] for sl in hs]
    for h in range(heads):
        oh = o[h] + jnp.dot(att[h].astype(BF16), v[h].astype(BF16), preferred_element_type=F32)
        if oprev_ref is not None:
            oh = oh + oprev_ref[:, hs[h]]
        osum_ref[:, hs[h]] = oh
    tail = (1 + mxu_level) * C
    for h in range(heads):
        whole = ex_ref[tail:tail + 1, hs[h]]
        ke = (k[h] * jnp.exp(whole - ex_ref[0:C, hs[h]])).astype(BF16)
        decay = jnp.exp(whole)
        st_ref[h] = decay * st[h] + jnp.dot(v[h].T.astype(BF16), ke, preferred_element_type=F32)
    if readout:
        gate = gate_ref[...]
        o_ref[...] = (_rms(osum_ref[...], gain_ref[...]) * (gate * _sigmoid(gate))).astype(o_ref.dtype)


def gla_scan(proj, *, batch, seq, ctx_len, reverse, o_prev=None, readout_gain=None):
    _, M, D = proj.shape
    C = GLA_CHUNK
    heads = min(GLA_HEADS_PER_STEP, D // HEAD_DIM)
    width = heads * HEAD_DIM
    readout = readout_gain is not None
    assert not readout or width == D, "the fused readout normalises whole rows"
    n_ctx, n_lat = ctx_len // C, seq // C
    ctx_base = batch * seq // C
    mcat, lvl = _gla_tables(reverse)
    mcat = jnp.asarray(mcat, BF16)
    lvl = jnp.asarray(lvl)
    g_section = 3 if reverse else 2

    def row_block(b, n):
        if reverse:
            return jnp.where(n < n_ctx, ctx_base + b * n_ctx + (n_ctx - 1 - n),
                             b * n_lat + (n_lat - 1 - (n - n_ctx)))
        return jnp.where(n < n_ctx, ctx_base + b * n_ctx + n, b * n_lat + (n - n_ctx))

    def section(s):
        return pl.BlockSpec((None, C, width), lambda b, hg, n: (s, row_block(b, n), hg))

    row = pl.BlockSpec((C, width), lambda b, hg, n: (row_block(b, n), hg))
    const = lambda shape: pl.BlockSpec(shape, lambda b, hg, n: (0, 0))
    args = [proj, proj, proj, mcat, lvl]
    in_specs = [section(0), section(1), section(g_section), const(mcat.shape), const(lvl.shape)]
    scratch = [pltpu.VMEM((heads, HEAD_DIM, HEAD_DIM), F32),
               pltpu.VMEM(mcat.shape[:1] + (width,), F32)]
    if o_prev is not None:
        args.append(o_prev)
        in_specs.append(row)
    if readout:
        args += [proj, readout_gain.reshape(1, D)]
        in_specs += [section(4), const((1, D))]
        scratch.append(pltpu.VMEM((C, width), F32))
    body = functools.partial(_gla_body, heads=heads, reverse=reverse, readout=readout)
    if o_prev is None:
        body = functools.partial(_gla_no_prev, body)
    return pl.pallas_call(
        body,
        grid=(batch, D // width, n_ctx + n_lat),
        in_specs=in_specs,
        out_specs=row,
        out_shape=jax.ShapeDtypeStruct((M, D), BF16 if readout else F32),
        scratch_shapes=scratch,
        input_output_aliases={5: 0} if (o_prev is not None and not readout) else {},
        compiler_params=_params(("parallel", "parallel", "arbitrary")),
        name="gla_bwd" if reverse else "gla_fwd",
    )(*args)


def _gla_no_prev(body, q_ref, v_ref, g_ref, mcat_ref, lvl_ref, *rest):
    body(q_ref, v_ref, g_ref, mcat_ref, lvl_ref, None, *rest)


def _rope_epilogue(acc, o_ref, cos_ref, sin_ref):
    cos, sin = cos_ref[...], sin_ref[...]
    lane = lax.broadcasted_iota(jnp.int32, cos.shape, 1)
    first_half = (lane % 32) < 16
    for hh in range(acc.shape[1] // V7X_LANES):
        sl = slice(hh * V7X_LANES, (hh + 1) * V7X_LANES)
        x = acc[:, sl]
        rot = jnp.where(first_half, -pltpu.roll(x, V7X_LANES - 16, 1), pltpu.roll(x, 16, 1))
        o_ref[:, sl] = (x * cos + rot * sin).astype(o_ref.dtype)


def rope_matmul(a, w, cos, sin, name):
    M = a.shape[0]
    tm = _tile(M, 1024, 8)
    tab = pl.BlockSpec((tm, V7X_LANES), lambda i, j, k: (i, 0))
    return matmul(a, w, out_dtype=BF16, epilogue=_rope_epilogue,
                  aux=((cos, tab), (sin, tab)), tm=tm, name=name)


def _attn_latent_body(qn_ref, qr_ref, knc_ref, krc_ref, vc_ref, knl_ref, krl_ref, vl_ref, o_ref,
                      kcat_ref, vaug_ref, *, ctx_len, chunk):
    @pl.when(pl.program_id(2) == 0)
    def _():
        kcat_ref[0:ctx_len, 0:HEAD_DIM] = knc_ref[...]
        kcat_ref[0:ctx_len, HEAD_DIM:] = krc_ref[...]
        kcat_ref[ctx_len:, 0:HEAD_DIM] = knl_ref[...]
        kcat_ref[ctx_len:, HEAD_DIM:] = krl_ref[...]
        lane = lax.broadcasted_iota(jnp.int32, (vaug_ref.shape[0], HEAD_DIM), 1)
        vaug_ref[:, HEAD_DIM:] = jnp.where(lane == 0, 1.0, 0.0).astype(BF16)
        vaug_ref[0:ctx_len, 0:HEAD_DIM] = vc_ref[...]
        vaug_ref[ctx_len:, 0:HEAD_DIM] = vl_ref[...]

    q = jnp.concatenate([qn_ref[...], qr_ref[...]], axis=1)
    total = kcat_ref.shape[0]
    spans = [(0, ctx_len)] + [(c, chunk) for c in range(ctx_len, total, chunk)]
    m = acc = None
    for start, size in spans:
        s = lax.dot_general(q, kcat_ref[start:start + size, :], _NT, preferred_element_type=F32)
        m_c = jnp.max(s, axis=-1, keepdims=True)
        m_new = m_c if m is None else jnp.maximum(m, m_c)
        p = jnp.exp2(s - m_new).astype(BF16)
        pv = jnp.dot(p, vaug_ref[start:start + size, :], preferred_element_type=F32)
        acc = pv if m is None else jnp.exp2(m - m_new) * acc + pv
        m = m_new
    o_ref[...] = (acc[:, :HEAD_DIM] / acc[:, HEAD_DIM:HEAD_DIM + 1]).astype(o_ref.dtype)


def _attn_context_body(qn_ref, qr_ref, kn_ref, kr_ref, v_ref, alias_ref, o_ref):
    del alias_ref
    s = (lax.dot_general(qn_ref[...], kn_ref[...], _NT, preferred_element_type=F32)
         + lax.dot_general(qr_ref[...], kr_ref[...], _NT, preferred_element_type=F32))
    p = jnp.exp2(s - jnp.max(s, axis=-1, keepdims=True))
    o = jnp.dot(p.astype(BF16), v_ref[...], preferred_element_type=F32)
    o_ref[...] = (o / jnp.sum(p, axis=-1, keepdims=True)).astype(o_ref.dtype)


def mla_attention(qn, qr, kn, kr, v, *, batch, seq, ctx_len, context_queries=True):
    M, D = qn.shape
    H = D // HEAD_DIM
    tq = _tile(seq, ATTN_Q_TILE, 8)
    nq = seq // tq
    chunk = _tile(seq, ATTN_KEY_CHUNK, 8)
    ctx_base = batch * seq // ctx_len
    blk = lambda rows, fn: pl.BlockSpec((rows, HEAD_DIM), fn)
    ctx_keys = [blk(ctx_len, lambda b, h, i: (ctx_base + b, h)),
                blk(ctx_len, lambda b, h, i: (ctx_base + b, 0)),
                blk(ctx_len, lambda b, h, i: (ctx_base + b, h))]
    lat_keys = [blk(seq, lambda b, h, i: (b, h)),
                blk(seq, lambda b, h, i: (b, 0)),
                blk(seq, lambda b, h, i: (b, h))]
    q_lat = blk(tq, lambda b, h, i: (b * nq + i, h))
    o_lat = pl.pallas_call(
        functools.partial(_attn_latent_body, ctx_len=ctx_len, chunk=chunk),
        grid=(batch, H, nq),
        in_specs=[q_lat, q_lat] + ctx_keys + lat_keys,
        out_specs=q_lat,
        out_shape=jax.ShapeDtypeStruct((M, D), BF16),
        scratch_shapes=[pltpu.VMEM((ctx_len + seq, 2 * HEAD_DIM), BF16),
                        pltpu.VMEM((ctx_len + seq, 2 * HEAD_DIM), BF16)],
        compiler_params=_params(("parallel", "parallel", "arbitrary")),
        name="mla_attention_latent",
    )(qn, qr, kn, kr, v, kn, kr, v)
    if not context_queries:
        return o_lat
    q_ctx = blk(ctx_len, lambda b, h, i: (ctx_base + b, h))
    return pl.pallas_call(
        _attn_context_body,
        grid=(batch, H, 1),
        in_specs=[q_ctx, q_ctx] + ctx_keys + [pl.BlockSpec(memory_space=pl.ANY)],
        out_specs=q_ctx,
        out_shape=jax.ShapeDtypeStruct((M, D), BF16),
        input_output_aliases={5: 0},
        compiler_params=_params(("parallel", "parallel", "arbitrary")),
        name="mla_attention_context",
    )(qn, qr, kn, kr, v, o_lat)


def _rope_tables(batch, seq, ctx_len):
    t = jnp.arange(seq)
    row = (t // GRID_W).astype(F32)
    col = (t % GRID_W).astype(F32)
    axis_dim = MLA_ROPE // 2
    inv_freq = ROPE_THETA ** (-jnp.arange(0, axis_dim, 2, dtype=F32) / axis_dim)
    ang_r = row[:, None] * inv_freq
    ang_c = col[:, None] * inv_freq
    ang = jnp.concatenate([ang_r, ang_r, ang_c, ang_c], axis=-1)
    pad = V7X_LANES - MLA_ROPE
    cos = jnp.pad(jnp.cos(ang), ((0, 0), (0, pad)), constant_values=1.0)
    sin = jnp.pad(jnp.sin(ang), ((0, 0), (0, pad)))
    n_ctx = batch * ctx_len
    cos = jnp.concatenate([jnp.tile(cos, (batch, 1)), jnp.ones((n_ctx, V7X_LANES), F32)], axis=0)
    sin = jnp.concatenate([jnp.tile(sin, (batch, 1)), jnp.zeros((n_ctx, V7X_LANES), F32)], axis=0)
    return cos, sin


def _pad_heads(w, heads, width):
    K = w.shape[0]
    w = w.reshape(K, heads, width)
    w = jnp.pad(w, ((0, 0), (0, 0), (0, V7X_LANES - width)))
    return w.reshape(K, heads * V7X_LANES).astype(BF16)


def kernel(x, c, ctx, c_ctx, ada_down, ada_up, ada_bias, norm_mix_pre, norm_mix_post, norm_mlp_pre,
           norm_mlp_post, mlp_w1, mlp_w2, hg_w_in, hg_lb_logits, hg_norm, hg_w_out, mla_w_down,
           mla_q_norm, mla_w_uq, mla_kv_norm, mla_w_ukv, mla_w_out):
    B, T, D = x.shape
    Lc = ctx.shape[1]
    depth = ada_down.shape[0]
    H = D // HEAD_DIM
    n_lat, n_ctx = B * T, B * Lc
    M = n_lat + n_ctx
    G = math.gcd(T, n_ctx)
    assert T % GLA_CHUNK == 0 and Lc % GLA_CHUNK == 0 and T % Lc == 0 and D % HEAD_DIM == 0

    cond = jnp.concatenate([c, c_ctx[None, :], jnp.zeros((7 - B % 8, D), F32)], axis=0)
    mods = ada_modulations(cond, ada_down, ada_up, ada_bias)
    group_src = np.concatenate([np.repeat(np.arange(B), T // G), np.full(n_ctx // G, B)])
    mods = mods[:, group_src, :].reshape(depth, M // G, 1, N_MOD, D)
    mod = lambda layer, idx: mods[layer, :, :, idx, :]

    lb = jnp.cumsum(jax.nn.softmax(hg_lb_logits.astype(F32), axis=0), axis=0)
    lb = lb - lb[:1]

    cos, sin = _rope_tables(B, T, Lc)
    hg_w_in_b = hg_w_in.astype(BF16)
    xs = jnp.concatenate([x.reshape(n_lat, D), ctx.reshape(n_ctx, D)], axis=0)
    h = norm_modulate(xs, norm_mix_pre[0], mod(0, 0), mod(0, 1), G)

    for layer in range(depth):
        j = layer // 2
        rows = n_lat if layer == depth - 1 else M
        if layer % 2 == 0:
            proj = hgrn2_project(h, hg_w_in_b[j], lb[j])
            o = gla_scan(proj, batch=B, seq=T, ctx_len=Lc, reverse=False)
            r = gla_scan(proj, batch=B, seq=T, ctx_len=Lc, reverse=True, o_prev=o,
                         readout_gain=hg_norm[j])
            y = matmul(r, hg_w_out[j].astype(BF16), out_dtype=F32, rows=rows, name="hgrn2_out")
        else:
            w_down = mla_w_down[j]
            n_lat_rank = MLA_Q_RANK + MLA_KV_RANK
            lat = matmul(h, w_down[:, :n_lat_rank].astype(BF16), out_dtype=F32, tn=512,
                         name="mla_down")
            kr = rope_matmul(h, _pad_heads(w_down[:, n_lat_rank:], 1, MLA_ROPE), cos, sin,
                             "mla_rope_key")
            cq, ckv = mla_latent_norms(lat, mla_q_norm[j], mla_kv_norm[j])
            w_uq = (mla_w_uq[j] * ATTN_EXP2_SCALE).reshape(MLA_Q_RANK, H, HEAD_DIM + MLA_ROPE)
            qn = matmul(cq, w_uq[:, :, :HEAD_DIM].reshape(MLA_Q_RANK, D).astype(BF16),
                        out_dtype=BF16, name="mla_q_nope")
            qr = rope_matmul(cq, _pad_heads(w_uq[:, :, HEAD_DIM:].reshape(MLA_Q_RANK, -1), H,
                                            MLA_ROPE), cos, sin, "mla_q_rope")
            w_ukv = mla_w_ukv[j].reshape(MLA_KV_RANK, H, 2 * HEAD_DIM)
            kn = matmul(ckv, w_ukv[:, :, :HEAD_DIM].reshape(MLA_KV_RANK, D).astype(BF16),
                        out_dtype=BF16, name="mla_k_nope")
            v = matmul(ckv, w_ukv[:, :, HEAD_DIM:].reshape(MLA_KV_RANK, D).astype(BF16),
                       out_dtype=BF16, name="mla_value")
            o = mla_attention(qn, qr, kn, kr, v, batch=B, seq=T, ctx_len=Lc,
                              context_queries=rows == M)
            y = matmul(o, mla_w_out[j].astype(BF16), out_dtype=F32, rows=rows, name="mla_out")
        xs, h = residual_norm(xs, y, norm_mix_post[layer], mod(layer, 2), G,
                              nxt=(norm_mlp_pre[layer], mod(layer, 3), mod(layer, 4)))
        hid = matmul(h, mlp_w1[layer].astype(BF16), out_dtype=BF16, epilogue=_epi_relu2,
                     name="mlp_up")
        m = matmul(hid, mlp_w2[layer].astype(BF16), out_dtype=F32, name="mlp_down")
        nxt = None
        if layer + 1 < depth:
            nxt = (norm_mix_pre[layer + 1], mod(layer + 1, 0), mod(layer + 1, 1))
        xs, h = residual_norm(xs, m, norm_mlp_post[layer], mod(layer, 5), G, nxt=nxt)
    return xs.reshape(B, T, D)
```

```python
import functools
import math

import numpy as np
import jax
import jax.numpy as jnp
from jax import lax
from jax.experimental import pallas as pl
from jax.experimental.pallas import tpu as pltpu

F32 = jnp.float32
BF16 = jnp.bfloat16

NORM_EPS = 1e-6
N_MOD = 6
GRID_W = 64
HEAD_DIM = 128
MLA_ROPE = 64
MLA_Q_RANK = 1024
MLA_KV_RANK = 512
MLA_SCALE = (HEAD_DIM + MLA_ROPE) ** -0.5
ATTN_EXP2_SCALE = MLA_SCALE * math.log2(math.e)
ATTN_Q_TILE = 2048
ATTN_KEY_CHUNK = 1024
ROPE_THETA = 10000.0
GLA_CHUNK = 64
GLA_HEADS_PER_STEP = 32
GLA_LEVELS = 6

V7X_LANES = 128
V7X_VMEM_LIMIT = 56 * 1024 * 1024

MM_FULL_K_MAX = 4096
MM_SPLIT_K = 4096

_NT = (((1,), (1,)), ((), ()))


def _tile(dim, pref, align=V7X_LANES):
    if dim <= pref:
        return dim
    t = (pref // align) * align
    while t >= align:
        if dim % t == 0:
            return t
        t -= align
    return dim


def _params(sem):
    return pltpu.CompilerParams(dimension_semantics=sem, vmem_limit_bytes=V7X_VMEM_LIMIT)


def _sigmoid(x):
    return 1.0 / (1.0 + jnp.exp(-x))


def _split3(x):
    hi = x.astype(BF16)
    r = x - hi.astype(F32)
    mid = r.astype(BF16)
    lo = (r - mid.astype(F32)).astype(BF16)
    return hi, mid, lo


def _mm_body(a_ref, b_ref, *rest, nk, n_aux, epilogue):
    aux = rest[:n_aux]
    o_ref = rest[n_aux]
    if nk == 1:
        epilogue(jnp.dot(a_ref[...], b_ref[...], preferred_element_type=F32), o_ref, *aux)
        return
    acc_ref = rest[n_aux + 1]
    k = pl.program_id(2)

    @pl.when(k == 0)
    def _():
        acc_ref[...] = jnp.dot(a_ref[...], b_ref[...], preferred_element_type=F32)

    @pl.when(k > 0)
    def _():
        acc_ref[...] += jnp.dot(a_ref[...], b_ref[...], preferred_element_type=F32)

    @pl.when(k == nk - 1)
    def _():
        epilogue(acc_ref[...], o_ref, *aux)


def _epi_plain(acc, o_ref):
    o_ref[...] = acc.astype(o_ref.dtype)


def _epi_relu2(acc, o_ref):
    r = jnp.maximum(acc, 0.0)
    o_ref[...] = (r * r).astype(o_ref.dtype)


def matmul(a, b, *, out_dtype, epilogue=_epi_plain, aux=(), out_shape=None, out_spec=None,
           tm=1024, tn=1024, tk=None, rows=None, name="mm"):
    K = a.shape[1]
    M = a.shape[0] if rows is None else rows
    _, N = b.shape
    if tk is None:
        tk = K if K <= MM_FULL_K_MAX else MM_SPLIT_K
    tm, tn, tk = _tile(M, tm, 8), _tile(N, tn), _tile(K, tk)
    nk = K // tk
    if out_shape is None:
        out_shape = jax.ShapeDtypeStruct((M, N), out_dtype)
        out_spec = pl.BlockSpec((tm, tn), lambda i, j, k: (i, j))
    in_specs = [pl.BlockSpec((tm, tk), lambda i, j, k: (i, k)),
                pl.BlockSpec((tk, tn), lambda i, j, k: (k, j))]
    in_specs += [spec for _, spec in aux]
    scratch = [] if nk == 1 else [pltpu.VMEM((tm, tn), F32)]
    return pl.pallas_call(
        functools.partial(_mm_body, nk=nk, n_aux=len(aux), epilogue=epilogue),
        grid=(M // tm, N // tn, nk),
        in_specs=in_specs,
        out_specs=out_spec,
        out_shape=out_shape,
        scratch_shapes=scratch,
        compiler_params=_params(("parallel", "parallel", "arbitrary")),
        name=name,
    )(a, b, *[arr for arr, _ in aux])


def _dot3(a, b):
    a0, a1, a2 = _split3(a)
    b0, b1, b2 = _split3(b)
    d = lambda x, y: jnp.dot(x, y, preferred_element_type=F32)
    return ((d(a0, b0) + (d(a0, b1) + d(a1, b0)))
            + ((d(a0, b2) + d(a2, b0)) + d(a1, b1)))


def _ada_body(cond_ref, down_ref, up_ref, bias_ref, o_ref, t_ref):
    @pl.when(pl.program_id(1) == 0)
    def _():
        cond = cond_ref[...]
        t_ref[...] = _dot3(cond * _sigmoid(cond), down_ref[...])

    o_ref[...] = _dot3(t_ref[...], up_ref[...]) + bias_ref[...]


def ada_modulations(cond, ada_down, ada_up, ada_bias):
    depth, D, rank = ada_down.shape
    R = cond.shape[0]
    N = ada_up.shape[2]
    tn = _tile(N, 2048)
    return pl.pallas_call(
        _ada_body,
        grid=(depth, N // tn),
        in_specs=[pl.BlockSpec((R, D), lambda l, j: (0, 0)),
                  pl.BlockSpec((None, D, rank), lambda l, j: (l, 0, 0)),
                  pl.BlockSpec((None, rank, tn), lambda l, j: (l, 0, j)),
                  pl.BlockSpec((None, 1, tn), lambda l, j: (l, 0, j))],
        out_specs=pl.BlockSpec((None, R, tn), lambda l, j: (l, 0, j)),
        out_shape=jax.ShapeDtypeStruct((depth, R, N), F32),
        scratch_shapes=[pltpu.VMEM((R, rank), F32)],
        compiler_params=_params(("parallel", "arbitrary")),
        name="ada_mod",
    )(cond, ada_down, ada_up, ada_bias.reshape(depth, 1, N))


def _rms(x, gain):
    ms = jnp.mean(x * x, axis=-1, keepdims=True)
    return x * lax.rsqrt(ms + NORM_EPS) * gain


def _pre_body(x_ref, g_ref, sh_ref, sc_ref, h_ref):
    h = _rms(x_ref[...], g_ref[...])
    h_ref[...] = (h * (1.0 + sc_ref[...]) + sh_ref[...]).astype(h_ref.dtype)


def norm_modulate(x, gain, shift, scale, group):
    M, D = x.shape
    tr = _tile(group, 256, 8)
    per = group // tr
    row = pl.BlockSpec((tr, D), lambda i: (i, 0))
    vec = pl.BlockSpec((1, D), lambda i: (0, 0))
    mod = pl.BlockSpec((None, 1, D), lambda i: (i // per, 0, 0))
    return pl.pallas_call(
        _pre_body,
        grid=(M // tr,),
        in_specs=[row, vec, mod, mod],
        out_specs=row,
        out_shape=jax.ShapeDtypeStruct((M, D), BF16),
        compiler_params=_params(("parallel",)),
        name="norm_modulate",
    )(x, gain.reshape(1, D), shift, scale)


def _post_body(x_ref, y_ref, gpost_ref, gate_ref, *rest, with_next):
    x = x_ref[...] + gate_ref[...] * _rms(y_ref[...], gpost_ref[...])
    if with_next:
        gpre_ref, sh_ref, sc_ref, xo_ref, h_ref = rest
        xo_ref[...] = x
        h = _rms(x, gpre_ref[...])
        h_ref[...] = (h * (1.0 + sc_ref[...]) + sh_ref[...]).astype(h_ref.dtype)
    else:
        (xo_ref,) = rest
        xo_ref[...] = x


def residual_norm(x, y, g_post, gate, group, nxt=None):
    M, D = y.shape
    tr = _tile(group, 256, 8)
    per = group // tr
    row = pl.BlockSpec((tr, D), lambda i: (i, 0))
    vec = pl.BlockSpec((1, D), lambda i: (0, 0))
    mod = pl.BlockSpec((None, 1, D), lambda i: (i // per, 0, 0))
    args = [x, y, g_post.reshape(1, D), gate]
    in_specs = [row, row, vec, mod]
    out_shape = [jax.ShapeDtypeStruct((M, D), F32)]
    out_specs = [row]
    if nxt is not None:
        gain, shift, scale = nxt
        args += [gain.reshape(1, D), shift, scale]
        in_specs += [vec, mod, mod]
        out_shape.append(jax.ShapeDtypeStruct((M, D), BF16))
        out_specs.append(row)
    out = pl.pallas_call(
        functools.partial(_post_body, with_next=nxt is not None),
        grid=(M // tr,),
        in_specs=in_specs,
        out_specs=out_specs,
        out_shape=out_shape,
        input_output_aliases={0: 0} if x.shape == y.shape else {},
        compiler_params=_params(("parallel",)),
        name="residual_norm",
    )(*args)
    return (out[0], out[1]) if nxt is not None else (out[0], None)


def _latent_norm_body(c_ref, qg_ref, kvg_ref, cq_ref, ckv_ref):
    c = c_ref[...]
    cq_ref[...] = _rms(c[:, :MLA_Q_RANK], qg_ref[...]).astype(cq_ref.dtype)
    ckv_ref[...] = _rms(c[:, MLA_Q_RANK:], kvg_ref[...]).astype(ckv_ref.dtype)


def mla_latent_norms(c, q_gain, kv_gain):
    M, W = c.shape
    tr = _tile(M, 512, 8)
    return pl.pallas_call(
        _latent_norm_body,
        grid=(M // tr,),
        in_specs=[pl.BlockSpec((tr, W), lambda i: (i, 0)),
                  pl.BlockSpec((1, MLA_Q_RANK), lambda i: (0, 0)),
                  pl.BlockSpec((1, MLA_KV_RANK), lambda i: (0, 0))],
        out_specs=[pl.BlockSpec((tr, MLA_Q_RANK), lambda i: (i, 0)),
                   pl.BlockSpec((tr, MLA_KV_RANK), lambda i: (i, 0))],
        out_shape=[jax.ShapeDtypeStruct((M, MLA_Q_RANK), BF16),
                   jax.ShapeDtypeStruct((M, MLA_KV_RANK), BF16)],
        compiler_params=_params(("parallel",)),
        name="mla_latent_norms",
    )(c, q_gain.reshape(1, -1), kv_gain.reshape(1, -1))


def _hg_proj_epilogue(acc, o_ref, lb_ref, omlb_ref, *, tiles_per_section):
    sec = pl.program_id(1) // tiles_per_section

    @pl.when(sec == 0)
    def _():
        o_ref[...] = acc * _sigmoid(acc)

    @pl.when(jnp.logical_or(sec == 1, sec == 4))
    def _():
        o_ref[...] = acc

    @pl.when(jnp.logical_or(sec == 2, sec == 3))
    def _():
        t = jnp.exp(-jnp.abs(acc))
        sig = jnp.where(acc >= 0.0, 1.0, t) / (1.0 + t)
        one_m_lb = omlb_ref[...]
        u = lb_ref[...] + one_m_lb * sig
        o_ref[...] = jnp.where(u > 0.0, jnp.log(u), acc + jnp.log(one_m_lb))


def hgrn2_project(h, w_in, lb, tn=1024):
    M, D = h.shape
    tm, tn = _tile(M, 1024, 8), _tile(D, tn)
    per = D // tn
    zeros = jnp.zeros((1, D), F32)
    aux_a = jnp.concatenate([zeros, zeros, lb, zeros], axis=0).reshape(5, 1, D)
    aux_b = jnp.concatenate([zeros, zeros, 1.0 - lb, zeros], axis=0).reshape(5, 1, D)
    aux_spec = pl.BlockSpec((None, 1, tn), lambda i, j, k: (j // per, 0, j % per))
    return matmul(
        h, w_in, out_dtype=F32,
        epilogue=functools.partial(_hg_proj_epilogue, tiles_per_section=per),
        aux=((aux_a, aux_spec), (aux_b, aux_spec)),
        out_shape=jax.ShapeDtypeStruct((5, M, D), F32),
        out_spec=pl.BlockSpec((None, tm, tn), lambda i, j, k: (j // per, i, j % per)),
        tm=tm, tn=tn, name="hgrn2_project")


def _gla_tables(reverse):
    C, L = GLA_CHUNK, GLA_LEVELS
    r = np.arange(C)
    mats = [(r[None, :] >= r[:, None]) if reverse else (r[None, :] <= r[:, None])]
    lvl = np.full((C, C), -1, np.int32)
    lvl[r, r] = L
    for l in range(L):
        w = C >> (l + 1)
        blk = r // w
        odd = (blk % 2) == 1
        rho = ((blk // 2) * 2 + 1) * w
        m = np.zeros((C, C), bool)
        for t in range(C):
            if not reverse:
                cols = (r >= rho[t]) & (r <= t) if odd[t] else (r > t) & (r < rho[t])
            else:
                cols = (r >= rho[t]) & (r < t) if odd[t] else (r >= t) & (r < rho[t])
            m[t] = cols
        if 2 <= w < 8:
            mats.append(m)
        same_parent = (r[:, None] // (2 * w)) == (r[None, :] // (2 * w))
        if not reverse:
            pair = same_parent & odd[:, None] & ~odd[None, :]
        else:
            pair = same_parent & ~odd[:, None] & odd[None, :]
        lvl[pair] = l
    tail = np.zeros((16, C), bool)
    tail[0] = True
    mats.append(tail)
    m = np.concatenate(mats, axis=0).astype(np.float32)
    return np.concatenate([m, m, m], axis=1), lvl


def _gla_body(q_ref, v_ref, g_ref, mcat_ref, lvl_ref, oprev_ref, *rest, heads, reverse, readout):
    if readout:
        gate_ref, gain_ref, o_ref, st_ref, ex_ref, osum_ref = rest
    else:
        o_ref, st_ref, ex_ref = rest
        osum_ref = o_ref
    C, L = GLA_CHUNK, GLA_LEVELS

    @pl.when(pl.program_id(2) == 0)
    def _():
        st_ref[...] = jnp.zeros_like(st_ref)

    g = g_ref[...]
    ex_ref[...] = jnp.dot(mcat_ref[...], jnp.concatenate(_split3(g), axis=0),
                          preferred_element_type=F32)
    lvl = lvl_ref[...]
    masks = [lvl == l for l in range(L + 1)]
    row = lax.broadcasted_iota(jnp.int32, (C, HEAD_DIM), 0)
    pair_query = (row % 2) == (0 if reverse else 1)
    hs = [slice(h * HEAD_DIM, (h + 1) * HEAD_DIM) for h in range(heads)]
    dot_nt = lambda a, b: lax.dot_general(a, b, _NT, preferred_element_type=F32)

    q = [q_ref[:, sl] for sl in hs]
    f = [jnp.exp(g[:, sl]) for sl in hs]
    k = [1.0 - fh for fh in f]
    qb = [x.astype(BF16) for x in q]
    kb = [x.astype(BF16) for x in k]
    st = [st_ref[h] for h in range(heads)]
    o = [dot_nt((q[h] * jnp.exp(ex_ref[0:C, hs[h]])).astype(BF16), st[h].astype(BF16))
         for h in range(heads)]
    att = [jnp.where(masks[L], dot_nt(qb[h], kb[h]), 0.0) for h in range(heads)]
    mxu_level = 0
    for l in range(L):
        w = C >> (l + 1)
        if w >= 8:
            src = [p + (w if reverse else w - 1) for p in range(0, C, 2 * w)]
            e = []
            for sl in hs:
                bound = jnp.concatenate(
                    [jnp.broadcast_to(ex_ref[r:r + 1, sl], (2 * w, HEAD_DIM)) for r in src], axis=0)
                e.append(jnp.exp(-jnp.abs(ex_ref[0:C, sl] - bound)).astype(BF16))
        elif w >= 2:
            lo = (1 + mxu_level) * C
            mxu_level += 1
            e = [jnp.exp(ex_ref[lo:lo + C, sl]).astype(BF16) for sl in hs]
        else:
            e = [jnp.where(pair_query, fh, 1.0).astype(BF16) for fh in f]
        p = [dot_nt(qb[h] * e[h], kb[h] * e[h]) for h in range(heads)]
        att = [jnp.where(masks[l], p[h], att[h]) for h in range(heads)]
    v = [v_ref[:, sl] for sl in hs]
    for h in range(heads):
        oh = o[h] + jnp.dot(att[h].astype(BF16), v[h].astype(BF16), preferred_element_type=F32)
        if oprev_ref is not None:
            oh = oh + oprev_ref[:, hs[h]]
        osum_ref[:, hs[h]] = oh
    tail = (1 + mxu_level) * C
    for h in range(heads):
        whole = ex_ref[tail:tail + 1, hs[h]]
        ke = (k[h] * jnp.exp(whole - ex_ref[0:C, hs[h]])).astype(BF16)
        decay = jnp.exp(whole)
        st_ref[h] = decay * st[h] + jnp.dot(v[h].T.astype(BF16), ke, preferred_element_type=F32)
    if readout:
        gate = gate_ref[...]
        o_ref[...] = (_rms(osum_ref[...], gain_ref[...]) * (gate * _sigmoid(gate))).astype(o_ref.dtype)


def gla_scan(proj, *, batch, seq, ctx_len, reverse, o_prev=None, readout_gain=None):
    _, M, D = proj.shape
    C = GLA_CHUNK
    heads = min(GLA_HEADS_PER_STEP, D // HEAD_DIM)
    width = heads * HEAD_DIM
    readout = readout_gain is not None
    assert not readout or width == D, "the fused readout normalises whole rows"
    n_ctx, n_lat = ctx_len // C, seq // C
    ctx_base = batch * seq // C
    mcat, lvl = _gla_tables(reverse)
    mcat = jnp.asarray(mcat, BF16)
    lvl = jnp.asarray(lvl)
    g_section = 3 if reverse else 2

    def row_block(b, n):
        if reverse:
            return jnp.where(n < n_ctx, ctx_base + b * n_ctx + (n_ctx - 1 - n),
                             b * n_lat + (n_lat - 1 - (n - n_ctx)))
        return jnp.where(n < n_ctx, ctx_base + b * n_ctx + n, b * n_lat + (n - n_ctx))

    def section(s):
        return pl.BlockSpec((None, C, width), lambda b, hg, n: (s, row_block(b, n), hg))

    row = pl.BlockSpec((C, width), lambda b, hg, n: (row_block(b, n), hg))
    const = lambda shape: pl.BlockSpec(shape, lambda b, hg, n: (0, 0))
    args = [proj, proj, proj, mcat, lvl]
    in_specs = [section(0), section(1), section(g_section), const(mcat.shape), const(lvl.shape)]
    scratch = [pltpu.VMEM((heads, HEAD_DIM, HEAD_DIM), F32),
               pltpu.VMEM(mcat.shape[:1] + (width,), F32)]
    if o_prev is not None:
        args.append(o_prev)
        in_specs.append(row)
    if readout:
        args += [proj, readout_gain.reshape(1, D)]
        in_specs += [section(4), const((1, D))]
        scratch.append(pltpu.VMEM((C, width), F32))
    body = functools.partial(_gla_body, heads=heads, reverse=reverse, readout=readout)
    if o_prev is None:
        body = functools.partial(_gla_no_prev, body)
    return pl.pallas_call(
        body,
        grid=(batch, D // width, n_ctx + n_lat),
        in_specs=in_specs,
        out_specs=row,
        out_shape=jax.ShapeDtypeStruct((M, D), BF16 if readout else F32),
        scratch_shapes=scratch,
        input_output_aliases={5: 0} if (o_prev is not None and not readout) else {},
        compiler_params=_params(("parallel", "parallel", "arbitrary")),
        name="gla_bwd" if reverse else "gla_fwd",
    )(*args)


def _gla_no_prev(body, q_ref, v_ref, g_ref, mcat_ref, lvl_ref, *rest):
    body(q_ref, v_ref, g_ref, mcat_ref, lvl_ref, None, *rest)


def _rope_epilogue(acc, o_ref, cos_ref, sin_ref, *, every):
    cos, sin = cos_ref[...], sin_ref[...]
    lane = lax.broadcasted_iota(jnp.int32, cos.shape, 1)
    first_half = (lane % 32) < 16
    for hh in range(acc.shape[1] // V7X_LANES):
        sl = slice(hh * V7X_LANES, (hh + 1) * V7X_LANES)
        x = acc[:, sl]
        if hh % every == every - 1:
            rot = jnp.where(first_half, -pltpu.roll(x, V7X_LANES - 16, 1), pltpu.roll(x, 16, 1))
            x = x * cos + rot * sin
        o_ref[:, sl] = x.astype(o_ref.dtype)


def rope_matmul(a, w, cos, sin, name, every=1):
    M = a.shape[0]
    tm = _tile(M, 1024, 8)
    tab = pl.BlockSpec((tm, V7X_LANES), lambda i, j, k: (i, 0))
    return matmul(a, w, out_dtype=BF16, epilogue=functools.partial(_rope_epilogue, every=every),
                  aux=((cos, tab), (sin, tab)), tm=tm, name=name)


def _attn_latent_body(q_ref, knc_ref, krc_ref, vc_ref, knl_ref, krl_ref, vl_ref, o_ref,
                      kcat_ref, vaug_ref, *, ctx_len, chunk):
    @pl.when(pl.program_id(2) == 0)
    def _():
        kcat_ref[0:ctx_len, 0:HEAD_DIM] = knc_ref[...]
        kcat_ref[0:ctx_len, HEAD_DIM:] = krc_ref[...]
        kcat_ref[ctx_len:, 0:HEAD_DIM] = knl_ref[...]
        kcat_ref[ctx_len:, HEAD_DIM:] = krl_ref[...]
        lane = lax.broadcasted_iota(jnp.int32, (vaug_ref.shape[0], HEAD_DIM), 1)
        vaug_ref[:, HEAD_DIM:] = jnp.where(lane == 0, 1.0, 0.0).astype(BF16)
        vaug_ref[0:ctx_len, 0:HEAD_DIM] = vc_ref[...]
        vaug_ref[ctx_len:, 0:HEAD_DIM] = vl_ref[...]

    q = q_ref[...]
    total = kcat_ref.shape[0]
    spans = [(0, ctx_len)] + [(c, chunk) for c in range(ctx_len, total, chunk)]
    m = acc = None
    for start, size in spans:
        s = lax.dot_general(q, kcat_ref[start:start + size, :], _NT, preferred_element_type=F32)
        m_c = jnp.max(s, axis=-1, keepdims=True)
        m_new = m_c if m is None else jnp.maximum(m, m_c)
        p = jnp.exp2(s - m_new).astype(BF16)
        pv = jnp.dot(p, vaug_ref[start:start + size, :], preferred_element_type=F32)
        acc = pv if m is None else jnp.exp2(m - m_new) * acc + pv
        m = m_new
    o_ref[...] = (acc[:, :HEAD_DIM] / acc[:, HEAD_DIM:HEAD_DIM + 1]).astype(o_ref.dtype)


def _attn_context_body(q_ref, kn_ref, kr_ref, v_ref, alias_ref, o_ref):
    del alias_ref
    q = q_ref[...]
    s = (lax.dot_general(q[:, :HEAD_DIM], kn_ref[...], _NT, preferred_element_type=F32)
         + lax.dot_general(q[:, HEAD_DIM:], kr_ref[...], _NT, preferred_element_type=F32))
    p = jnp.exp2(s - jnp.max(s, axis=-1, keepdims=True))
    o = jnp.dot(p.astype(BF16), v_ref[...], preferred_element_type=F32)
    o_ref[...] = (o / jnp.sum(p, axis=-1, keepdims=True)).astype(o_ref.dtype)


def mla_attention(q, kn, kr, v, *, batch, seq, ctx_len, context_queries=True):
    M, D = kn.shape
    H = D // HEAD_DIM
    tq = _tile(seq, ATTN_Q_TILE, 8)
    nq = seq // tq
    chunk = _tile(seq, ATTN_KEY_CHUNK, 8)
    ctx_base = batch * seq // ctx_len
    blk = lambda rows, fn: pl.BlockSpec((rows, HEAD_DIM), fn)
    ctx_keys = [blk(ctx_len, lambda b, h, i: (ctx_base + b, h)),
                blk(ctx_len, lambda b, h, i: (ctx_base + b, 0)),
                blk(ctx_len, lambda b, h, i: (ctx_base + b, h))]
    lat_keys = [blk(seq, lambda b, h, i: (b, h)),
                blk(seq, lambda b, h, i: (b, 0)),
                blk(seq, lambda b, h, i: (b, h))]
    q_lat = pl.BlockSpec((tq, 2 * HEAD_DIM), lambda b, h, i: (b * nq + i, h))
    o_lat_spec = blk(tq, lambda b, h, i: (b * nq + i, h))
    o_lat = pl.pallas_call(
        functools.partial(_attn_latent_body, ctx_len=ctx_len, chunk=chunk),
        grid=(batch, H, nq),
        in_specs=[q_lat] + ctx_keys + lat_keys,
        out_specs=o_lat_spec,
        out_shape=jax.ShapeDtypeStruct((M, D), BF16),
        scratch_shapes=[pltpu.VMEM((ctx_len + seq, 2 * HEAD_DIM), BF16),
                        pltpu.VMEM((ctx_len + seq, 2 * HEAD_DIM), BF16)],
        compiler_params=_params(("parallel", "parallel", "arbitrary")),
        name="mla_attention_latent",
    )(q, kn, kr, v, kn, kr, v)
    if not context_queries:
        return o_lat
    q_ctx = pl.BlockSpec((ctx_len, 2 * HEAD_DIM), lambda b, h, i: (ctx_base + b, h))
    o_ctx = blk(ctx_len, lambda b, h, i: (ctx_base + b, h))
    return pl.pallas_call(
        _attn_context_body,
        grid=(batch, H, 1),
        in_specs=[q_ctx] + ctx_keys + [pl.BlockSpec(memory_space=pl.ANY)],
        out_specs=o_ctx,
        out_shape=jax.ShapeDtypeStruct((M, D), BF16),
        input_output_aliases={4: 0},
        compiler_params=_params(("parallel", "parallel", "arbitrary")),
        name="mla_attention_context",
    )(q, kn, kr, v, o_lat)


def _rope_tables(batch, seq, ctx_len):
    t = jnp.arange(seq)
    row = (t // GRID_W).astype(F32)
    col = (t % GRID_W).astype(F32)
    axis_dim = MLA_ROPE // 2
    inv_freq = ROPE_THETA ** (-jnp.arange(0, axis_dim, 2, dtype=F32) / axis_dim)
    ang_r = row[:, None] * inv_freq
    ang_c = col[:, None] * inv_freq
    ang = jnp.concatenate([ang_r, ang_r, ang_c, ang_c], axis=-1)
    pad = V7X_LANES - MLA_ROPE
    cos = jnp.pad(jnp.cos(ang), ((0, 0), (0, pad)), constant_values=1.0)
    sin = jnp.pad(jnp.sin(ang), ((0, 0), (0, pad)))
    n_ctx = batch * ctx_len
    cos = jnp.concatenate([jnp.tile(cos, (batch, 1)), jnp.ones((n_ctx, V7X_LANES), F32)], axis=0)
    sin = jnp.concatenate([jnp.tile(sin, (batch, 1)), jnp.zeros((n_ctx, V7X_LANES), F32)], axis=0)
    return cos, sin


def _pad_heads(w, heads, width):
    K = w.shape[0]
    w = w.reshape(K, heads, width)
    w = jnp.pad(w, ((0, 0), (0, 0), (0, V7X_LANES - width)))
    return w.reshape(K, heads * V7X_LANES).astype(BF16)


def kernel(x, c, ctx, c_ctx, ada_down, ada_up, ada_bias, norm_mix_pre, norm_mix_post, norm_mlp_pre,
           norm_mlp_post, mlp_w1, mlp_w2, hg_w_in, hg_lb_logits, hg_norm, hg_w_out, mla_w_down,
           mla_q_norm, mla_w_uq, mla_kv_norm, mla_w_ukv, mla_w_out):
    B, T, D = x.shape
    Lc = ctx.shape[1]
    depth = ada_down.shape[0]
    H = D // HEAD_DIM
    n_lat, n_ctx = B * T, B * Lc
    M = n_lat + n_ctx
    G = math.gcd(T, n_ctx)
    assert T % GLA_CHUNK == 0 and Lc % GLA_CHUNK == 0 and T % Lc == 0 and D % HEAD_DIM == 0

    cond = jnp.concatenate([c, c_ctx[None, :], jnp.zeros((7 - B % 8, D), F32)], axis=0)
    mods = ada_modulations(cond, ada_down, ada_up, ada_bias)
    group_src = np.concatenate([np.repeat(np.arange(B), T // G), np.full(n_ctx // G, B)])
    mods = mods[:, group_src, :].reshape(depth, M // G, 1, N_MOD, D)
    mod = lambda layer, idx: mods[layer, :, :, idx, :]

    lb = jnp.cumsum(jax.nn.softmax(hg_lb_logits.astype(F32), axis=0), axis=0)
    lb = lb - lb[:1]

    cos, sin = _rope_tables(B, T, Lc)
    hg_w_in_b = hg_w_in.astype(BF16)
    xs = jnp.concatenate([x.reshape(n_lat, D), ctx.reshape(n_ctx, D)], axis=0)
    h = norm_modulate(xs, norm_mix_pre[0], mod(0, 0), mod(0, 1), G)

    for layer in range(depth):
        j = layer // 2
        rows = n_lat if layer == depth - 1 else M
        if layer % 2 == 0:
            proj = hgrn2_project(h, hg_w_in_b[j], lb[j])
            o = gla_scan(proj, batch=B, seq=T, ctx_len=Lc, reverse=False)
            r = gla_scan(proj, batch=B, seq=T, ctx_len=Lc, reverse=True, o_prev=o,
                         readout_gain=hg_norm[j])
            y = matmul(r, hg_w_out[j].astype(BF16), out_dtype=F32, rows=rows, name="hgrn2_out")
        else:
            w_down = mla_w_down[j]
            n_lat_rank = MLA_Q_RANK + MLA_KV_RANK
            lat = matmul(h, w_down[:, :n_lat_rank].astype(BF16), out_dtype=F32, tn=512,
                         name="mla_down")
            kr = rope_matmul(h, _pad_heads(w_down[:, n_lat_rank:], 1, MLA_ROPE), cos, sin,
                             "mla_rope_key")
            cq, ckv = mla_latent_norms(lat, mla_q_norm[j], mla_kv_norm[j])
            w_uq = (mla_w_uq[j] * ATTN_EXP2_SCALE).reshape(MLA_Q_RANK, H, HEAD_DIM + MLA_ROPE)
            w_q = jnp.pad(w_uq, ((0, 0), (0, 0), (0, 2 * HEAD_DIM - w_uq.shape[2])))
            q = rope_matmul(cq, w_q.reshape(MLA_Q_RANK, 2 * D).astype(BF16), cos, sin, "mla_q",
                            every=2)
            w_ukv = mla_w_ukv[j].reshape(MLA_KV_RANK, H, 2 * HEAD_DIM)
            kn = matmul(ckv, w_ukv[:, :, :HEAD_DIM].reshape(MLA_KV_RANK, D).astype(BF16),
                        out_dtype=BF16, name="mla_k_nope")
            v = matmul(ckv, w_ukv[:, :, HEAD_DIM:].reshape(MLA_KV_RANK, D).astype(BF16),
                       out_dtype=BF16, name="mla_value")
            o = mla_attention(q, kn, kr, v, batch=B, seq=T, ctx_len=Lc,
                              context_queries=rows == M)
            y = matmul(o, mla_w_out[j].astype(BF16), out_dtype=F32, rows=rows, name="mla_out")
        xs, h = residual_norm(xs, y, norm_mix_post[layer], mod(layer, 2), G,
                              nxt=(norm_mlp_pre[layer], mod(layer, 3), mod(layer, 4)))
        hid = matmul(h, mlp_w1[layer].astype(BF16), out_dtype=BF16, epilogue=_epi_relu2,
                     name="mlp_up")
        m = matmul(hid, mlp_w2[layer].astype(BF16), out_dtype=F32, name="mlp_down")
        nxt = None
        if layer + 1 < depth:
            nxt = (norm_mix_pre[layer + 1], mod(layer + 1, 0), mod(layer + 1, 1))
        xs, h = residual_norm(xs, m, norm_mlp_post[layer], mod(layer, 5), G, nxt=nxt)
    return xs.reshape(B, T, D)
```

```python
import functools
import math

import numpy as np
import jax
import jax.numpy as jnp
from jax import lax
from jax.experimental import pallas as pl
from jax.experimental.pallas import tpu as pltpu

F32 = jnp.float32
BF16 = jnp.bfloat16

NORM_EPS = 1e-6
N_MOD = 6
GRID_W = 64
HEAD_DIM = 128
MLA_ROPE = 64
MLA_Q_RANK = 1024
MLA_KV_RANK = 512
MLA_SCALE = (HEAD_DIM + MLA_ROPE) ** -0.5
ATTN_EXP2_SCALE = MLA_SCALE * math.log2(math.e)
ATTN_Q_TILE = 2048
ATTN_KEY_CHUNK = 1024
ROPE_THETA = 10000.0
GLA_CHUNK = 64
GLA_HEADS_PER_STEP = 32
GLA_VPU_MIN_W = 4
GLA_LEVELS = 6

V7X_LANES = 128
V7X_VMEM_LIMIT = 56 * 1024 * 1024

MM_FULL_K_MAX = 4096
MM_SPLIT_K = 4096

_NT = (((1,), (1,)), ((), ()))


def _tile(dim, pref, align=V7X_LANES):
    if dim <= pref:
        return dim
    t = (pref // align) * align
    while t >= align:
        if dim % t == 0:
            return t
        t -= align
    return dim


def _params(sem):
    return pltpu.CompilerParams(dimension_semantics=sem, vmem_limit_bytes=V7X_VMEM_LIMIT)


def _sigmoid(x):
    return 1.0 / (1.0 + jnp.exp(-x))


def _split3(x):
    hi = x.astype(BF16)
    r = x - hi.astype(F32)
    mid = r.astype(BF16)
    lo = (r - mid.astype(F32)).astype(BF16)
    return hi, mid, lo


def _mm_body(a_ref, b_ref, *rest, nk, n_aux, epilogue):
    aux = rest[:n_aux]
    o_ref = rest[n_aux]
    if nk == 1:
        epilogue(jnp.dot(a_ref[...], b_ref[...], preferred_element_type=F32), o_ref, *aux)
        return
    acc_ref = rest[n_aux + 1]
    k = pl.program_id(2)

    @pl.when(k == 0)
    def _():
        acc_ref[...] = jnp.dot(a_ref[...], b_ref[...], preferred_element_type=F32)

    @pl.when(k > 0)
    def _():
        acc_ref[...] += jnp.dot(a_ref[...], b_ref[...], preferred_element_type=F32)

    @pl.when(k == nk - 1)
    def _():
        epilogue(acc_ref[...], o_ref, *aux)


def _epi_plain(acc, o_ref):
    o_ref[...] = acc.astype(o_ref.dtype)


def _epi_relu2(acc, o_ref):
    r = jnp.maximum(acc, 0.0)
    o_ref[...] = (r * r).astype(o_ref.dtype)


def matmul(a, b, *, out_dtype, epilogue=_epi_plain, aux=(), out_shape=None, out_spec=None,
           tm=1024, tn=1024, tk=None, rows=None, layer=None, name="mm"):
    K = a.shape[1]
    M = a.shape[0] if rows is None else rows
    N = b.shape[-1]
    if tk is None:
        tk = K if K <= MM_FULL_K_MAX else MM_SPLIT_K
    tm, tn, tk = _tile(M, tm, 8), _tile(N, tn), _tile(K, tk)
    nk = K // tk
    if out_shape is None:
        out_shape = jax.ShapeDtypeStruct((M, N), out_dtype)
        out_spec = pl.BlockSpec((tm, tn), lambda i, j, k: (i, j))
    if layer is None:
        b_spec = pl.BlockSpec((tk, tn), lambda i, j, k: (k, j))
    else:
        b_spec = pl.BlockSpec((None, tk, tn), lambda i, j, k: (layer, k, j))
    in_specs = [pl.BlockSpec((tm, tk), lambda i, j, k: (i, k)), b_spec]
    in_specs += [spec for _, spec in aux]
    scratch = [] if nk == 1 else [pltpu.VMEM((tm, tn), F32)]
    return pl.pallas_call(
        functools.partial(_mm_body, nk=nk, n_aux=len(aux), epilogue=epilogue),
        grid=(M // tm, N // tn, nk),
        in_specs=in_specs,
        out_specs=out_spec,
        out_shape=out_shape,
        scratch_shapes=scratch,
        compiler_params=_params(("parallel", "parallel", "arbitrary")),
        name=name,
    )(a, b, *[arr for arr, _ in aux])


def _dot3(a, b):
    a0, a1, a2 = _split3(a)
    b0, b1, b2 = _split3(b)
    d = lambda x, y: jnp.dot(x, y, preferred_element_type=F32)
    return ((d(a0, b0) + (d(a0, b1) + d(a1, b0)))
            + ((d(a0, b2) + d(a2, b0)) + d(a1, b1)))


def _ada_body(cond_ref, down_ref, up_ref, bias_ref, o_ref, t_ref):
    @pl.when(pl.program_id(1) == 0)
    def _():
        cond = cond_ref[...]
        t_ref[...] = _dot3(cond * _sigmoid(cond), down_ref[...])

    o_ref[...] = _dot3(t_ref[...], up_ref[...]) + bias_ref[...]


def ada_modulations(cond, ada_down, ada_up, ada_bias):
    depth, D, rank = ada_down.shape
    R = cond.shape[0]
    N = ada_up.shape[2]
    tn = _tile(N, 2048)
    return pl.pallas_call(
        _ada_body,
        grid=(depth, N // tn),
        in_specs=[pl.BlockSpec((R, D), lambda l, j: (0, 0)),
                  pl.BlockSpec((None, D, rank), lambda l, j: (l, 0, 0)),
                  pl.BlockSpec((None, rank, tn), lambda l, j: (l, 0, j)),
                  pl.BlockSpec((None, 1, tn), lambda l, j: (l, 0, j))],
        out_specs=pl.BlockSpec((None, R, tn), lambda l, j: (l, 0, j)),
        out_shape=jax.ShapeDtypeStruct((depth, R, N), F32),
        scratch_shapes=[pltpu.VMEM((R, rank), F32)],
        compiler_params=_params(("parallel", "arbitrary")),
        name="ada_mod",
    )(cond, ada_down, ada_up, ada_bias.reshape(depth, 1, N))


def _rms(x, gain):
    ms = jnp.mean(x * x, axis=-1, keepdims=True)
    return x * lax.rsqrt(ms + NORM_EPS) * gain


def _pre_body(x_ref, g_ref, sh_ref, sc_ref, h_ref):
    h = _rms(x_ref[...], g_ref[...])
    h_ref[...] = (h * (1.0 + sc_ref[...]) + sh_ref[...]).astype(h_ref.dtype)


def norm_modulate(x, gain, shift, scale, group):
    M, D = x.shape
    tr = _tile(group, 256, 8)
    per = group // tr
    row = pl.BlockSpec((tr, D), lambda i: (i, 0))
    vec = pl.BlockSpec((1, D), lambda i: (0, 0))
    mod = pl.BlockSpec((None, 1, D), lambda i: (i // per, 0, 0))
    return pl.pallas_call(
        _pre_body,
        grid=(M // tr,),
        in_specs=[row, vec, mod, mod],
        out_specs=row,
        out_shape=jax.ShapeDtypeStruct((M, D), BF16),
        compiler_params=_params(("parallel",)),
        name="norm_modulate",
    )(x, gain.reshape(1, D), shift, scale)


def _post_body(x_ref, y_ref, gpost_ref, gate_ref, *rest, with_next):
    x = x_ref[...] + gate_ref[...] * _rms(y_ref[...], gpost_ref[...])
    if with_next:
        gpre_ref, sh_ref, sc_ref, xo_ref, h_ref = rest
        xo_ref[...] = x
        h = _rms(x, gpre_ref[...])
        h_ref[...] = (h * (1.0 + sc_ref[...]) + sh_ref[...]).astype(h_ref.dtype)
    else:
        (xo_ref,) = rest
        xo_ref[...] = x


def residual_norm(x, y, g_post, gate, group, nxt=None):
    M, D = y.shape
    tr = _tile(group, 256, 8)
    per = group // tr
    row = pl.BlockSpec((tr, D), lambda i: (i, 0))
    vec = pl.BlockSpec((1, D), lambda i: (0, 0))
    mod = pl.BlockSpec((None, 1, D), lambda i: (i // per, 0, 0))
    args = [x, y, g_post.reshape(1, D), gate]
    in_specs = [row, row, vec, mod]
    out_shape = [jax.ShapeDtypeStruct((M, D), F32)]
    out_specs = [row]
    if nxt is not None:
        gain, shift, scale = nxt
        args += [gain.reshape(1, D), shift, scale]
        in_specs += [vec, mod, mod]
        out_shape.append(jax.ShapeDtypeStruct((M, D), BF16))
        out_specs.append(row)
    out = pl.pallas_call(
        functools.partial(_post_body, with_next=nxt is not None),
        grid=(M // tr,),
        in_specs=in_specs,
        out_specs=out_specs,
        out_shape=out_shape,
        input_output_aliases={0: 0} if x.shape == y.shape else {},
        compiler_params=_params(("parallel",)),
        name="residual_norm",
    )(*args)
    return (out[0], out[1]) if nxt is not None else (out[0], None)


def _latent_norm_body(c_ref, qg_ref, kvg_ref, cq_ref, ckv_ref):
    c = c_ref[...]
    cq_ref[...] = _rms(c[:, :MLA_Q_RANK], qg_ref[...]).astype(cq_ref.dtype)
    ckv_ref[...] = _rms(c[:, MLA_Q_RANK:], kvg_ref[...]).astype(ckv_ref.dtype)


def mla_latent_norms(c, q_gain, kv_gain):
    M, W = c.shape
    tr = _tile(M, 512, 8)
    return pl.pallas_call(
        _latent_norm_body,
        grid=(M // tr,),
        in_specs=[pl.BlockSpec((tr, W), lambda i: (i, 0)),
                  pl.BlockSpec((1, MLA_Q_RANK), lambda i: (0, 0)),
                  pl.BlockSpec((1, MLA_KV_RANK), lambda i: (0, 0))],
        out_specs=[pl.BlockSpec((tr, MLA_Q_RANK), lambda i: (i, 0)),
                   pl.BlockSpec((tr, MLA_KV_RANK), lambda i: (i, 0))],
        out_shape=[jax.ShapeDtypeStruct((M, MLA_Q_RANK), BF16),
                   jax.ShapeDtypeStruct((M, MLA_KV_RANK), BF16)],
        compiler_params=_params(("parallel",)),
        name="mla_latent_norms",
    )(c, q_gain.reshape(1, -1), kv_gain.reshape(1, -1))


def _hg_proj_epilogue(acc, o_ref, lb_ref, omlb_ref, *, tiles_per_section):
    sec = pl.program_id(1) // tiles_per_section

    @pl.when(sec == 0)
    def _():
        o_ref[...] = acc * _sigmoid(acc)

    @pl.when(jnp.logical_or(sec == 1, sec == 4))
    def _():
        o_ref[...] = acc

    @pl.when(jnp.logical_or(sec == 2, sec == 3))
    def _():
        t = jnp.exp(-jnp.abs(acc))
        sig = jnp.where(acc >= 0.0, 1.0, t) / (1.0 + t)
        one_m_lb = omlb_ref[...]
        u = lb_ref[...] + one_m_lb * sig
        o_ref[...] = jnp.where(u > 0.0, jnp.log(u), acc + jnp.log(one_m_lb))


def hgrn2_project(h, w_in, layer, lb, tn=1024):
    M, D = h.shape
    tm, tn = _tile(M, 1024, 8), _tile(D, tn)
    per = D // tn
    zeros = jnp.zeros((1, D), F32)
    aux_a = jnp.concatenate([zeros, zeros, lb, zeros], axis=0).reshape(5, 1, D)
    aux_b = jnp.concatenate([zeros, zeros, 1.0 - lb, zeros], axis=0).reshape(5, 1, D)
    aux_spec = pl.BlockSpec((None, 1, tn), lambda i, j, k: (j // per, 0, j % per))
    return matmul(
        h, w_in, out_dtype=F32,
        epilogue=functools.partial(_hg_proj_epilogue, tiles_per_section=per),
        aux=((aux_a, aux_spec), (aux_b, aux_spec)),
        out_shape=jax.ShapeDtypeStruct((5, M, D), F32),
        out_spec=pl.BlockSpec((None, tm, tn), lambda i, j, k: (j // per, i, j % per)),
        tm=tm, tn=tn, layer=layer, name="hgrn2_project")


def _gla_tables(reverse):
    C, L = GLA_CHUNK, GLA_LEVELS
    r = np.arange(C)
    mats = [(r[None, :] >= r[:, None]) if reverse else (r[None, :] <= r[:, None])]
    lvl = np.full((C, C), -1, np.int32)
    lvl[r, r] = L
    for l in range(L):
        w = C >> (l + 1)
        blk = r // w
        odd = (blk % 2) == 1
        rho = ((blk // 2) * 2 + 1) * w
        m = np.zeros((C, C), bool)
        for t in range(C):
            if not reverse:
                cols = (r >= rho[t]) & (r <= t) if odd[t] else (r > t) & (r < rho[t])
            else:
                cols = (r >= rho[t]) & (r < t) if odd[t] else (r >= t) & (r < rho[t])
            m[t] = cols
        if 2 <= w < GLA_VPU_MIN_W:
            mats.append(m)
        same_parent = (r[:, None] // (2 * w)) == (r[None, :] // (2 * w))
        if not reverse:
            pair = same_parent & odd[:, None] & ~odd[None, :]
        else:
            pair = same_parent & ~odd[:, None] & odd[None, :]
        lvl[pair] = l
    tail = np.zeros((16, C), bool)
    tail[0] = True
    mats.append(tail)
    m = np.concatenate(mats, axis=0).astype(np.float32)
    return np.concatenate([m, m, m], axis=1), lvl


def _gla_body(q_ref, v_ref, g_ref, mcat_ref, lvl_ref, oprev_ref, *rest, heads, reverse, readout):
    if readout:
        gate_ref, gain_ref, o_ref, st_ref, ex_ref, osum_ref = rest
    else:
        o_ref, st_ref, ex_ref = rest
        osum_ref = o_ref
    C, L = GLA_CHUNK, GLA_LEVELS

    @pl.when(pl.program_id(2) == 0)
    def _():
        st_ref[...] = jnp.zeros_like(st_ref)

    g = g_ref[...]
    ex_ref[...] = jnp.dot(mcat_ref[...], jnp.concatenate(_split3(g), axis=0),
                          preferred_element_type=F32)
    lvl = lvl_ref[...]
    masks = [lvl == l for l in range(L + 1)]
    row = lax.broadcasted_iota(jnp.int32, (C, HEAD_DIM), 0)
    pair_query = (row % 2) == (0 if reverse else 1)
    hs = [slice(h * HEAD_DIM, (h + 1) * HEAD_DIM) for h in range(heads)]
    dot_nt = lambda a, b: lax.dot_general(a, b, _NT, preferred_element_type=F32)

    q = [q_ref[:, sl] for sl in hs]
    f = [jnp.exp(g[:, sl]) for sl in hs]
    k = [1.0 - fh for fh in f]
    qb = [x.astype(BF16) for x in q]
    kb = [x.astype(BF16) for x in k]
    st = [st_ref[h] for h in range(heads)]
    o = [dot_nt((q[h] * jnp.exp(ex_ref[0:C, hs[h]])).astype(BF16), st[h].astype(BF16))
         for h in range(heads)]
    att = [jnp.where(masks[L], dot_nt(qb[h], kb[h]), 0.0) for h in range(heads)]
    mxu_level = 0
    for l in range(L):
        w = C >> (l + 1)
        if w >= GLA_VPU_MIN_W:
            src = [p + (w if reverse else w - 1) for p in range(0, C, 2 * w)]
            e = []
            for sl in hs:
                bound = jnp.concatenate(
                    [jnp.broadcast_to(ex_ref[r:r + 1, sl], (2 * w, HEAD_DIM)) for r in src], axis=0)
                e.append(jnp.exp(-jnp.abs(ex_ref[0:C, sl] - bound)).astype(BF16))
        elif w >= 2:
            lo = (1 + mxu_level) * C
            mxu_level += 1
            e = [jnp.exp(ex_ref[lo:lo + C, sl]).astype(BF16) for sl in hs]
        else:
            e = [jnp.where(pair_query, fh, 1.0).astype(BF16) for fh in f]
        p = [dot_nt(qb[h] * e[h], kb[h] * e[h]) for h in range(heads)]
        att = [jnp.where(masks[l], p[h], att[h]) for h in range(heads)]
    v = [v_ref[:, sl] for sl in hs]
    for h in range(heads):
        oh = o[h] + jnp.dot(att[h].astype(BF16), v[h].astype(BF16), preferred_element_type=F32)
        if oprev_ref is not None:
            oh = oh + oprev_ref[:, hs[h]]
        osum_ref[:, hs[h]] = oh
    tail = (1 + mxu_level) * C
    for h in range(heads):
        whole = ex_ref[tail:tail + 1, hs[h]]
        ke = (k[h] * jnp.exp(whole - ex_ref[0:C, hs[h]])).astype(BF16)
        decay = jnp.exp(whole)
        st_ref[h] = decay * st[h] + jnp.dot(v[h].T.astype(BF16), ke, preferred_element_type=F32)
    if readout:
        gate = gate_ref[...]
        o_ref[...] = (_rms(osum_ref[...], gain_ref[...]) * (gate * _sigmoid(gate))).astype(o_ref.dtype)


def gla_scan(proj, *, batch, seq, ctx_len, reverse, o_prev=None, readout_gain=None):
    _, M, D = proj.shape
    C = GLA_CHUNK
    heads = min(GLA_HEADS_PER_STEP, D // HEAD_DIM)
    width = heads * HEAD_DIM
    readout = readout_gain is not None
    assert not readout or width == D, "the fused readout normalises whole rows"
    n_ctx, n_lat = ctx_len // C, seq // C
    ctx_base = batch * seq // C
    mcat, lvl = _gla_tables(reverse)
    mcat = jnp.asarray(mcat, BF16)
    lvl = jnp.asarray(lvl)
    g_section = 3 if reverse else 2

    def row_block(b, n):
        if reverse:
            return jnp.where(n < n_ctx, ctx_base + b * n_ctx + (n_ctx - 1 - n),
                             b * n_lat + (n_lat - 1 - (n - n_ctx)))
        return jnp.where(n < n_ctx, ctx_base + b * n_ctx + n, b * n_lat + (n - n_ctx))

    def section(s):
        return pl.BlockSpec((None, C, width), lambda b, hg, n: (s, row_block(b, n), hg))

    row = pl.BlockSpec((C, width), lambda b, hg, n: (row_block(b, n), hg))
    const = lambda shape: pl.BlockSpec(shape, lambda b, hg, n: (0, 0))
    args = [proj, proj, proj, mcat, lvl]
    in_specs = [section(0), section(1), section(g_section), const(mcat.shape), const(lvl.shape)]
    scratch = [pltpu.VMEM((heads, HEAD_DIM, HEAD_DIM), F32),
               pltpu.VMEM(mcat.shape[:1] + (width,), F32)]
    if o_prev is not None:
        args.append(o_prev)
        in_specs.append(row)
    if readout:
        args += [proj, readout_gain.reshape(1, D)]
        in_specs += [section(4), const((1, D))]
        scratch.append(pltpu.VMEM((C, width), F32))
    body = functools.partial(_gla_body, heads=heads, reverse=reverse, readout=readout)
    if o_prev is None:
        body = functools.partial(_gla_no_prev, body)
    return pl.pallas_call(
        body,
        grid=(batch, D // width, n_ctx + n_lat),
        in_specs=in_specs,
        out_specs=row,
        out_shape=jax.ShapeDtypeStruct((M, D), BF16 if readout else F32),
        scratch_shapes=scratch,
        input_output_aliases={5: 0} if (o_prev is not None and not readout) else {},
        compiler_params=_params(("parallel", "parallel", "arbitrary")),
        name="gla_bwd" if reverse else "gla_fwd",
    )(*args)


def _gla_no_prev(body, q_ref, v_ref, g_ref, mcat_ref, lvl_ref, *rest):
    body(q_ref, v_ref, g_ref, mcat_ref, lvl_ref, None, *rest)


def _rope_epilogue(acc, o_ref, cos_ref, sin_ref, *, every):
    cos, sin = cos_ref[...], sin_ref[...]
    lane = lax.broadcasted_iota(jnp.int32, cos.shape, 1)
    first_half = (lane % 32) < 16
    for hh in range(acc.shape[1] // V7X_LANES):
        sl = slice(hh * V7X_LANES, (hh + 1) * V7X_LANES)
        x = acc[:, sl]
        if hh % every == every - 1:
            rot = jnp.where(first_half, -pltpu.roll(x, V7X_LANES - 16, 1), pltpu.roll(x, 16, 1))
            x = x * cos + rot * sin
        o_ref[:, sl] = x.astype(o_ref.dtype)


def rope_matmul(a, w, cos, sin, name, every=1):
    M = a.shape[0]
    tm = _tile(M, 1024, 8)
    tab = pl.BlockSpec((tm, V7X_LANES), lambda i, j, k: (i, 0))
    return matmul(a, w, out_dtype=BF16, epilogue=functools.partial(_rope_epilogue, every=every),
                  aux=((cos, tab), (sin, tab)), tm=tm, name=name)


def _attn_latent_body(q_ref, knc_ref, krc_ref, vc_ref, knl_ref, krl_ref, vl_ref, o_ref,
                      kcat_ref, vaug_ref, *, ctx_len, chunk):
    @pl.when(pl.program_id(2) == 0)
    def _():
        kcat_ref[0:ctx_len, 0:HEAD_DIM] = knc_ref[...]
        kcat_ref[0:ctx_len, HEAD_DIM:] = krc_ref[...]
        kcat_ref[ctx_len:, 0:HEAD_DIM] = knl_ref[...]
        kcat_ref[ctx_len:, HEAD_DIM:] = krl_ref[...]
        lane = lax.broadcasted_iota(jnp.int32, (vaug_ref.shape[0], HEAD_DIM), 1)
        vaug_ref[:, HEAD_DIM:] = jnp.where(lane == 0, 1.0, 0.0).astype(BF16)
        vaug_ref[0:ctx_len, 0:HEAD_DIM] = vc_ref[...]
        vaug_ref[ctx_len:, 0:HEAD_DIM] = vl_ref[...]

    q = q_ref[...]
    total = kcat_ref.shape[0]
    spans = [(0, ctx_len)] + [(c, chunk) for c in range(ctx_len, total, chunk)]
    m = acc = None
    for start, size in spans:
        s = lax.dot_general(q, kcat_ref[start:start + size, :], _NT, preferred_element_type=F32)
        m_c = jnp.max(s, axis=-1, keepdims=True)
        m_new = m_c if m is None else jnp.maximum(m, m_c)
        p = jnp.exp2(s - m_new).astype(BF16)
        pv = jnp.dot(p, vaug_ref[start:start + size, :], preferred_element_type=F32)
        acc = pv if m is None else jnp.exp2(m - m_new) * acc + pv
        m = m_new
    o_ref[...] = (acc[:, :HEAD_DIM] / acc[:, HEAD_DIM:HEAD_DIM + 1]).astype(o_ref.dtype)


def _attn_context_body(q_ref, kn_ref, kr_ref, v_ref, alias_ref, o_ref):
    del alias_ref
    q = q_ref[...]
    s = (lax.dot_general(q[:, :HEAD_DIM], kn_ref[...], _NT, preferred_element_type=F32)
         + lax.dot_general(q[:, HEAD_DIM:], kr_ref[...], _NT, preferred_element_type=F32))
    p = jnp.exp2(s - jnp.max(s, axis=-1, keepdims=True))
    o = jnp.dot(p.astype(BF16), v_ref[...], preferred_element_type=F32)
    o_ref[...] = (o / jnp.sum(p, axis=-1, keepdims=True)).astype(o_ref.dtype)


def mla_attention(q, kn, kr, v, *, batch, seq, ctx_len, context_queries=True):
    M, D = kn.shape
    H = D // HEAD_DIM
    tq = _tile(seq, ATTN_Q_TILE, 8)
    nq = seq // tq
    chunk = _tile(seq, ATTN_KEY_CHUNK, 8)
    ctx_base = batch * seq // ctx_len
    blk = lambda rows, fn: pl.BlockSpec((rows, HEAD_DIM), fn)
    ctx_keys = [blk(ctx_len, lambda b, h, i: (ctx_base + b, h)),
                blk(ctx_len, lambda b, h, i: (ctx_base + b, 0)),
                blk(ctx_len, lambda b, h, i: (ctx_base + b, h))]
    lat_keys = [blk(seq, lambda b, h, i: (b, h)),
                blk(seq, lambda b, h, i: (b, 0)),
                blk(seq, lambda b, h, i: (b, h))]
    q_lat = pl.BlockSpec((tq, 2 * HEAD_DIM), lambda b, h, i: (b * nq + i, h))
    o_lat_spec = blk(tq, lambda b, h, i: (b * nq + i, h))
    o_lat = pl.pallas_call(
        functools.partial(_attn_latent_body, ctx_len=ctx_len, chunk=chunk),
        grid=(batch, H, nq),
        in_specs=[q_lat] + ctx_keys + lat_keys,
        out_specs=o_lat_spec,
        out_shape=jax.ShapeDtypeStruct((M, D), BF16),
        scratch_shapes=[pltpu.VMEM((ctx_len + seq, 2 * HEAD_DIM), BF16),
                        pltpu.VMEM((ctx_len + seq, 2 * HEAD_DIM), BF16)],
        compiler_params=_params(("parallel", "parallel", "arbitrary")),
        name="mla_attention_latent",
    )(q, kn, kr, v, kn, kr, v)
    if not context_queries:
        return o_lat
    q_ctx = pl.BlockSpec((ctx_len, 2 * HEAD_DIM), lambda b, h, i: (ctx_base + b, h))
    o_ctx = blk(ctx_len, lambda b, h, i: (ctx_base + b, h))
    return pl.pallas_call(
        _attn_context_body,
        grid=(batch, H, 1),
        in_specs=[q_ctx] + ctx_keys + [pl.BlockSpec(memory_space=pl.ANY)],
        out_specs=o_ctx,
        out_shape=jax.ShapeDtypeStruct((M, D), BF16),
        input_output_aliases={4: 0},
        compiler_params=_params(("parallel", "parallel", "arbitrary")),
        name="mla_attention_context",
    )(q, kn, kr, v, o_lat)


def _rope_tables(batch, seq, ctx_len):
    t = jnp.arange(seq)
    row = (t // GRID_W).astype(F32)
    col = (t % GRID_W).astype(F32)
    axis_dim = MLA_ROPE // 2
    inv_freq = ROPE_THETA ** (-jnp.arange(0, axis_dim, 2, dtype=F32) / axis_dim)
    ang_r = row[:, None] * inv_freq
    ang_c = col[:, None] * inv_freq
    ang = jnp.concatenate([ang_r, ang_r, ang_c, ang_c], axis=-1)
    pad = V7X_LANES - MLA_ROPE
    cos = jnp.pad(jnp.cos(ang), ((0, 0), (0, pad)), constant_values=1.0)
    sin = jnp.pad(jnp.sin(ang), ((0, 0), (0, pad)))
    n_ctx = batch * ctx_len
    cos = jnp.concatenate([jnp.tile(cos, (batch, 1)), jnp.ones((n_ctx, V7X_LANES), F32)], axis=0)
    sin = jnp.concatenate([jnp.tile(sin, (batch, 1)), jnp.zeros((n_ctx, V7X_LANES), F32)], axis=0)
    return cos, sin


def _pad_heads(w, heads, width):
    K = w.shape[0]
    w = w.reshape(K, heads, width)
    w = jnp.pad(w, ((0, 0), (0, 0), (0, V7X_LANES - width)))
    return w.reshape(K, heads * V7X_LANES).astype(BF16)


def kernel(x, c, ctx, c_ctx, ada_down, ada_up, ada_bias, norm_mix_pre, norm_mix_post, norm_mlp_pre,
           norm_mlp_post, mlp_w1, mlp_w2, hg_w_in, hg_lb_logits, hg_norm, hg_w_out, mla_w_down,
           mla_q_norm, mla_w_uq, mla_kv_norm, mla_w_ukv, mla_w_out):
    B, T, D = x.shape
    Lc = ctx.shape[1]
    depth = ada_down.shape[0]
    H = D // HEAD_DIM
    n_lat, n_ctx = B * T, B * Lc
    M = n_lat + n_ctx
    G = math.gcd(T, n_ctx)
    assert T % GLA_CHUNK == 0 and Lc % GLA_CHUNK == 0 and T % Lc == 0 and D % HEAD_DIM == 0

    cond = jnp.concatenate([c, c_ctx[None, :], jnp.zeros((7 - B % 8, D), F32)], axis=0)
    mods = ada_modulations(cond, ada_down, ada_up, ada_bias)
    group_src = np.concatenate([np.repeat(np.arange(B), T // G), np.full(n_ctx // G, B)])
    mods = mods[:, group_src, :].reshape(depth, M // G, 1, N_MOD, D)
    mod = lambda layer, idx: mods[layer, :, :, idx, :]

    lb = jnp.cumsum(jax.nn.softmax(hg_lb_logits.astype(F32), axis=0), axis=0)
    lb = lb - lb[:1]

    cos, sin = _rope_tables(B, T, Lc)
    hg_w_in_b, hg_w_out_b, mla_w_out_b = (w.astype(BF16) for w in (hg_w_in, hg_w_out, mla_w_out))
    mlp_w1_b, mlp_w2_b = mlp_w1.astype(BF16), mlp_w2.astype(BF16)
    xs = jnp.concatenate([x.reshape(n_lat, D), ctx.reshape(n_ctx, D)], axis=0)
    h = norm_modulate(xs, norm_mix_pre[0], mod(0, 0), mod(0, 1), G)

    for layer in range(depth):
        j = layer // 2
        rows = n_lat if layer == depth - 1 else M
        if layer % 2 == 0:
            proj = hgrn2_project(h, hg_w_in_b, j, lb[j])
            o = gla_scan(proj, batch=B, seq=T, ctx_len=Lc, reverse=False)
            r = gla_scan(proj, batch=B, seq=T, ctx_len=Lc, reverse=True, o_prev=o,
                         readout_gain=hg_norm[j])
            y = matmul(r, hg_w_out_b, out_dtype=F32, rows=rows, layer=j, name="hgrn2_out")
        else:
            w_down = mla_w_down[j]
            n_lat_rank = MLA_Q_RANK + MLA_KV_RANK
            lat = matmul(h, w_down[:, :n_lat_rank].astype(BF16), out_dtype=F32, tn=512,
                         name="mla_down")
            kr = rope_matmul(h, _pad_heads(w_down[:, n_lat_rank:], 1, MLA_ROPE), cos, sin,
                             "mla_rope_key")
            cq, ckv = mla_latent_norms(lat, mla_q_norm[j], mla_kv_norm[j])
            w_uq = (mla_w_uq[j] * ATTN_EXP2_SCALE).reshape(MLA_Q_RANK, H, HEAD_DIM + MLA_ROPE)
            w_q = jnp.pad(w_uq, ((0, 0), (0, 0), (0, 2 * HEAD_DIM - w_uq.shape[2])))
            q = rope_matmul(cq, w_q.reshape(MLA_Q_RANK, 2 * D).astype(BF16), cos, sin, "mla_q",
                            every=2)
            w_ukv = mla_w_ukv[j].reshape(MLA_KV_RANK, H, 2 * HEAD_DIM)
            kn = matmul(ckv, w_ukv[:, :, :HEAD_DIM].reshape(MLA_KV_RANK, D).astype(BF16),
                        out_dtype=BF16, name="mla_k_nope")
            v = matmul(ckv, w_ukv[:, :, HEAD_DIM:].reshape(MLA_KV_RANK, D).astype(BF16),
                       out_dtype=BF16, name="mla_value")
            o = mla_attention(q, kn, kr, v, batch=B, seq=T, ctx_len=Lc,
                              context_queries=rows == M)
            y = matmul(o, mla_w_out_b, out_dtype=F32, rows=rows, layer=j, name="mla_out")
        xs, h = residual_norm(xs, y, norm_mix_post[layer], mod(layer, 2), G,
                              nxt=(norm_mlp_pre[layer], mod(layer, 3), mod(layer, 4)))
        hid = matmul(h, mlp_w1_b, out_dtype=BF16, epilogue=_epi_relu2, layer=layer, name="mlp_up")
        m = matmul(hid, mlp_w2_b, out_dtype=F32, layer=layer, name="mlp_down")
        nxt = None
        if layer + 1 < depth:
            nxt = (norm_mix_pre[layer + 1], mod(layer + 1, 0), mod(layer + 1, 1))
        xs, h = residual_norm(xs, m, norm_mlp_post[layer], mod(layer, 5), G, nxt=nxt)
    return xs.reshape(B, T, D)
```

```python
import functools
import math

import numpy as np
import jax
import jax.numpy as jnp
from jax import lax
from jax.experimental import pallas as pl
from jax.experimental.pallas import tpu as pltpu

F32 = jnp.float32
BF16 = jnp.bfloat16

NORM_EPS = 1e-6
N_MOD = 6
GRID_W = 64
HEAD_DIM = 128
MLA_ROPE = 64
MLA_Q_RANK = 1024
MLA_KV_RANK = 512
MLA_SCALE = (HEAD_DIM + MLA_ROPE) ** -0.5
ATTN_EXP2_SCALE = MLA_SCALE * math.log2(math.e)
ATTN_Q_TILE = 4096
ATTN_KEY_CHUNK = 1024
ROPE_THETA = 10000.0
GLA_CHUNK = 64
GLA_HEADS_PER_STEP = 32
GLA_VPU_MIN_W = 4
GLA_LEVELS = 6

V7X_LANES = 128
V7X_VMEM_LIMIT = 56 * 1024 * 1024

MM_FULL_K_MAX = 4096
MM_SPLIT_K = 4096

_NT = (((1,), (1,)), ((), ()))


def _tile(dim, pref, align=V7X_LANES):
    if dim <= pref:
        return dim
    t = (pref // align) * align
    while t >= align:
        if dim % t == 0:
            return t
        t -= align
    return dim


def _params(sem):
    return pltpu.CompilerParams(dimension_semantics=sem, vmem_limit_bytes=V7X_VMEM_LIMIT)


def _sigmoid(x):
    return 1.0 / (1.0 + jnp.exp(-x))


def _split3(x):
    hi = x.astype(BF16)
    r = x - hi.astype(F32)
    mid = r.astype(BF16)
    lo = (r - mid.astype(F32)).astype(BF16)
    return hi, mid, lo


def _mm_body(a_ref, b_ref, *rest, nk, n_aux, epilogue):
    aux = rest[:n_aux]
    o_ref = rest[n_aux]
    if nk == 1:
        epilogue(jnp.dot(a_ref[...], b_ref[...], preferred_element_type=F32), o_ref, *aux)
        return
    acc_ref = rest[n_aux + 1]
    k = pl.program_id(2)

    @pl.when(k == 0)
    def _():
        acc_ref[...] = jnp.dot(a_ref[...], b_ref[...], preferred_element_type=F32)

    @pl.when(k > 0)
    def _():
        acc_ref[...] += jnp.dot(a_ref[...], b_ref[...], preferred_element_type=F32)

    @pl.when(k == nk - 1)
    def _():
        epilogue(acc_ref[...], o_ref, *aux)


def _epi_plain(acc, o_ref):
    o_ref[...] = acc.astype(o_ref.dtype)


def _epi_relu2(acc, o_ref):
    r = jnp.maximum(acc, 0.0)
    o_ref[...] = (r * r).astype(o_ref.dtype)


def matmul(a, b, *, out_dtype, epilogue=_epi_plain, aux=(), out_shape=None, out_spec=None,
           tm=1024, tn=1024, tk=None, rows=None, layer=None, name="mm"):
    K = a.shape[1]
    M = a.shape[0] if rows is None else rows
    N = b.shape[-1]
    if tk is None:
        tk = K if K <= MM_FULL_K_MAX else MM_SPLIT_K
    tm, tn, tk = _tile(M, tm, 8), _tile(N, tn), _tile(K, tk)
    nk = K // tk
    if out_shape is None:
        out_shape = jax.ShapeDtypeStruct((M, N), out_dtype)
        out_spec = pl.BlockSpec((tm, tn), lambda i, j, k: (i, j))
    if layer is None:
        b_spec = pl.BlockSpec((tk, tn), lambda i, j, k: (k, j))
    else:
        b_spec = pl.BlockSpec((None, tk, tn), lambda i, j, k: (layer, k, j))
    in_specs = [pl.BlockSpec((tm, tk), lambda i, j, k: (i, k)), b_spec]
    in_specs += [spec for _, spec in aux]
    scratch = [] if nk == 1 else [pltpu.VMEM((tm, tn), F32)]
    return pl.pallas_call(
        functools.partial(_mm_body, nk=nk, n_aux=len(aux), epilogue=epilogue),
        grid=(M // tm, N // tn, nk),
        in_specs=in_specs,
        out_specs=out_spec,
        out_shape=out_shape,
        scratch_shapes=scratch,
        compiler_params=_params(("parallel", "parallel", "arbitrary")),
        name=name,
    )(a, b, *[arr for arr, _ in aux])


def _dot3(a, b):
    a0, a1, a2 = _split3(a)
    b0, b1, b2 = _split3(b)
    d = lambda x, y: jnp.dot(x, y, preferred_element_type=F32)
    return ((d(a0, b0) + (d(a0, b1) + d(a1, b0)))
            + ((d(a0, b2) + d(a2, b0)) + d(a1, b1)))


def _ada_body(cond_ref, down_ref, up_ref, bias_ref, o_ref, t_ref):
    @pl.when(pl.program_id(1) == 0)
    def _():
        cond = cond_ref[...]
        t_ref[...] = _dot3(cond * _sigmoid(cond), down_ref[...])

    o_ref[...] = _dot3(t_ref[...], up_ref[...]) + bias_ref[...]


def ada_modulations(cond, ada_down, ada_up, ada_bias):
    depth, D, rank = ada_down.shape
    R = cond.shape[0]
    N = ada_up.shape[2]
    tn = _tile(N, 2048)
    return pl.pallas_call(
        _ada_body,
        grid=(depth, N // tn),
        in_specs=[pl.BlockSpec((R, D), lambda l, j: (0, 0)),
                  pl.BlockSpec((None, D, rank), lambda l, j: (l, 0, 0)),
                  pl.BlockSpec((None, rank, tn), lambda l, j: (l, 0, j)),
                  pl.BlockSpec((None, 1, tn), lambda l, j: (l, 0, j))],
        out_specs=pl.BlockSpec((None, R, tn), lambda l, j: (l, 0, j)),
        out_shape=jax.ShapeDtypeStruct((depth, R, N), F32),
        scratch_shapes=[pltpu.VMEM((R, rank), F32)],
        compiler_params=_params(("parallel", "arbitrary")),
        name="ada_mod",
    )(cond, ada_down, ada_up, ada_bias.reshape(depth, 1, N))


def _rms(x, gain):
    ms = jnp.mean(x * x, axis=-1, keepdims=True)
    return x * lax.rsqrt(ms + NORM_EPS) * gain


def _embed_body(lat_ref, ctx_ref, g_ref, sh_ref, sc_ref, xs_ref, h_ref, *, lat_blocks):
    def emit(src_ref):
        x = src_ref[...]
        xs_ref[...] = x
        h_ref[...] = (_rms(x, g_ref[...]) * (1.0 + sc_ref[...]) + sh_ref[...]).astype(h_ref.dtype)

    is_latent = pl.program_id(0) < lat_blocks
    pl.when(is_latent)(lambda: emit(lat_ref))
    pl.when(jnp.logical_not(is_latent))(lambda: emit(ctx_ref))


def embed_norm_modulate(lat, ctx, gain, shift, scale, group):
    n_lat, D = lat.shape
    M = n_lat + ctx.shape[0]
    tr = _tile(group, 256, 8)
    per = group // tr
    lat_blocks = n_lat // tr
    row = pl.BlockSpec((tr, D), lambda i: (i, 0))
    vec = pl.BlockSpec((1, D), lambda i: (0, 0))
    mod = pl.BlockSpec((None, 1, D), lambda i: (i // per, 0, 0))
    return pl.pallas_call(
        functools.partial(_embed_body, lat_blocks=lat_blocks),
        grid=(M // tr,),
        in_specs=[pl.BlockSpec((tr, D), lambda i: (jnp.minimum(i, lat_blocks - 1), 0)),
                  pl.BlockSpec((tr, D), lambda i: (jnp.maximum(i - lat_blocks, 0), 0)),
                  vec, mod, mod],
        out_specs=[row, row],
        out_shape=[jax.ShapeDtypeStruct((M, D), F32), jax.ShapeDtypeStruct((M, D), BF16)],
        compiler_params=_params(("arbitrary",)),
        name="embed_norm_modulate",
    )(lat, ctx, gain.reshape(1, D), shift, scale)


def _post_body(x_ref, y_ref, gpost_ref, gate_ref, *rest, with_next):
    x = x_ref[...] + gate_ref[...] * _rms(y_ref[...], gpost_ref[...])
    if with_next:
        gpre_ref, sh_ref, sc_ref, xo_ref, h_ref = rest
        xo_ref[...] = x
        h = _rms(x, gpre_ref[...])
        h_ref[...] = (h * (1.0 + sc_ref[...]) + sh_ref[...]).astype(h_ref.dtype)
    else:
        (xo_ref,) = rest
        xo_ref[...] = x


def residual_norm(x, y, g_post, gate, group, nxt=None):
    M, D = y.shape
    tr = _tile(group, 256, 8)
    per = group // tr
    row = pl.BlockSpec((tr, D), lambda i: (i, 0))
    vec = pl.BlockSpec((1, D), lambda i: (0, 0))
    mod = pl.BlockSpec((None, 1, D), lambda i: (i // per, 0, 0))
    args = [x, y, g_post.reshape(1, D), gate]
    in_specs = [row, row, vec, mod]
    out_shape = [jax.ShapeDtypeStruct((M, D), F32)]
    out_specs = [row]
    if nxt is not None:
        gain, shift, scale = nxt
        args += [gain.reshape(1, D), shift, scale]
        in_specs += [vec, mod, mod]
        out_shape.append(jax.ShapeDtypeStruct((M, D), BF16))
        out_specs.append(row)
    out = pl.pallas_call(
        functools.partial(_post_body, with_next=nxt is not None),
        grid=(M // tr,),
        in_specs=in_specs,
        out_specs=out_specs,
        out_shape=out_shape,
        input_output_aliases={0: 0} if x.shape == y.shape else {},
        compiler_params=_params(("parallel",)),
        name="residual_norm",
    )(*args)
    return (out[0], out[1]) if nxt is not None else (out[0], None)


def _latent_norm_body(c_ref, qg_ref, kvg_ref, cq_ref, ckv_ref):
    c = c_ref[...]
    cq_ref[...] = _rms(c[:, :MLA_Q_RANK], qg_ref[...]).astype(cq_ref.dtype)
    ckv_ref[...] = _rms(c[:, MLA_Q_RANK:], kvg_ref[...]).astype(ckv_ref.dtype)


def mla_latent_norms(c, q_gain, kv_gain):
    M, W = c.shape
    tr = _tile(M, 512, 8)
    return pl.pallas_call(
        _latent_norm_body,
        grid=(M // tr,),
        in_specs=[pl.BlockSpec((tr, W), lambda i: (i, 0)),
                  pl.BlockSpec((1, MLA_Q_RANK), lambda i: (0, 0)),
                  pl.BlockSpec((1, MLA_KV_RANK), lambda i: (0, 0))],
        out_specs=[pl.BlockSpec((tr, MLA_Q_RANK), lambda i: (i, 0)),
                   pl.BlockSpec((tr, MLA_KV_RANK), lambda i: (i, 0))],
        out_shape=[jax.ShapeDtypeStruct((M, MLA_Q_RANK), BF16),
                   jax.ShapeDtypeStruct((M, MLA_KV_RANK), BF16)],
        compiler_params=_params(("parallel",)),
        name="mla_latent_norms",
    )(c, q_gain.reshape(1, -1), kv_gain.reshape(1, -1))


def _hg_proj_epilogue(acc, o_ref, lb_ref, omlb_ref, *, tiles_per_section):
    sec = pl.program_id(1) // tiles_per_section

    @pl.when(sec == 0)
    def _():
        o_ref[...] = acc * _sigmoid(acc)

    @pl.when(jnp.logical_or(sec == 1, sec == 4))
    def _():
        o_ref[...] = acc

    @pl.when(jnp.logical_or(sec == 2, sec == 3))
    def _():
        t = jnp.exp(-jnp.abs(acc))
        sig = jnp.where(acc >= 0.0, 1.0, t) / (1.0 + t)
        one_m_lb = omlb_ref[...]
        u = lb_ref[...] + one_m_lb * sig
        o_ref[...] = jnp.where(u > 0.0, jnp.log(u), acc + jnp.log(one_m_lb))


def hgrn2_project(h, w_in, layer, lb, tn=1024):
    M, D = h.shape
    tm, tn = _tile(M, 1024, 8), _tile(D, tn)
    per = D // tn
    zeros = jnp.zeros((1, D), F32)
    aux_a = jnp.concatenate([zeros, zeros, lb, zeros], axis=0).reshape(5, 1, D)
    aux_b = jnp.concatenate([zeros, zeros, 1.0 - lb, zeros], axis=0).reshape(5, 1, D)
    aux_spec = pl.BlockSpec((None, 1, tn), lambda i, j, k: (j // per, 0, j % per))
    return matmul(
        h, w_in, out_dtype=F32,
        epilogue=functools.partial(_hg_proj_epilogue, tiles_per_section=per),
        aux=((aux_a, aux_spec), (aux_b, aux_spec)),
        out_shape=jax.ShapeDtypeStruct((5, M, D), F32),
        out_spec=pl.BlockSpec((None, tm, tn), lambda i, j, k: (j // per, i, j % per)),
        tm=tm, tn=tn, layer=layer, name="hgrn2_project")


def _gla_tables(reverse):
    C, L = GLA_CHUNK, GLA_LEVELS
    r = np.arange(C)
    mats = [(r[None, :] >= r[:, None]) if reverse else (r[None, :] <= r[:, None])]
    lvl = np.full((C, C), -1, np.int32)
    lvl[r, r] = L
    for l in range(L):
        w = C >> (l + 1)
        blk = r // w
        odd = (blk % 2) == 1
        rho = ((blk // 2) * 2 + 1) * w
        m = np.zeros((C, C), bool)
        for t in range(C):
            if not reverse:
                cols = (r >= rho[t]) & (r <= t) if odd[t] else (r > t) & (r < rho[t])
            else:
                cols = (r >= rho[t]) & (r < t) if odd[t] else (r >= t) & (r < rho[t])
            m[t] = cols
        if 2 <= w < GLA_VPU_MIN_W:
            mats.append(m)
        same_parent = (r[:, None] // (2 * w)) == (r[None, :] // (2 * w))
        if not reverse:
            pair = same_parent & odd[:, None] & ~odd[None, :]
        else:
            pair = same_parent & ~odd[:, None] & odd[None, :]
        lvl[pair] = l
    tail = np.zeros((16, C), bool)
    tail[0] = True
    mats.append(tail)
    m = np.concatenate(mats, axis=0).astype(np.float32)
    return np.concatenate([m, m, m], axis=1), lvl


def _gla_body(q_ref, v_ref, g_ref, mcat_ref, lvl_ref, oprev_ref, *rest, heads, reverse, readout):
    if readout:
        gate_ref, gain_ref, o_ref, st_ref, ex_ref, osum_ref = rest
    else:
        o_ref, st_ref, ex_ref = rest
        osum_ref = o_ref
    C, L = GLA_CHUNK, GLA_LEVELS

    @pl.when(pl.program_id(2) == 0)
    def _():
        st_ref[...] = jnp.zeros_like(st_ref)

    g = g_ref[...]
    ex_ref[...] = jnp.dot(mcat_ref[...], jnp.concatenate(_split3(g), axis=0),
                          preferred_element_type=F32)
    lvl = lvl_ref[...]
    masks = [lvl == l for l in range(L + 1)]
    row = lax.broadcasted_iota(jnp.int32, (C, HEAD_DIM), 0)
    pair_query = (row % 2) == (0 if reverse else 1)
    hs = [slice(h * HEAD_DIM, (h + 1) * HEAD_DIM) for h in range(heads)]
    dot_nt = lambda a, b: lax.dot_general(a, b, _NT, preferred_element_type=F32)

    q = [q_ref[:, sl] for sl in hs]
    f = [jnp.exp(g[:, sl]) for sl in hs]
    k = [1.0 - fh for fh in f]
    qb = [x.astype(BF16) for x in q]
    kb = [x.astype(BF16) for x in k]
    st = [st_ref[h] for h in range(heads)]
    o = [dot_nt((q[h] * jnp.exp(ex_ref[0:C, hs[h]])).astype(BF16), st[h].astype(BF16))
         for h in range(heads)]
    att = [jnp.where(masks[L], dot_nt(qb[h], kb[h]), 0.0) for h in range(heads)]
    mxu_level = 0
    for l in range(L):
        w = C >> (l + 1)
        if w >= GLA_VPU_MIN_W:
            src = [p + (w if reverse else w - 1) for p in range(0, C, 2 * w)]
            e = []
            for sl in hs:
                bound = jnp.concatenate(
                    [jnp.broadcast_to(ex_ref[r:r + 1, sl], (2 * w, HEAD_DIM)) for r in src], axis=0)
                e.append(jnp.exp(-jnp.abs(ex_ref[0:C, sl] - bound)).astype(BF16))
        elif w >= 2:
            lo = (1 + mxu_level) * C
            mxu_level += 1
            e = [jnp.exp(ex_ref[lo:lo + C, sl]).astype(BF16) for sl in hs]
        else:
            e = [jnp.where(pair_query, fh, 1.0).astype(BF16) for fh in f]
        p = [dot_nt(qb[h] * e[h], kb[h] * e[h]) for h in range(heads)]
        att = [jnp.where(masks[l], p[h], att[h]) for h in range(heads)]
    v = [v_ref[:, sl] for sl in hs]
    for h in range(heads):
        oh = o[h] + jnp.dot(att[h].astype(BF16), v[h].astype(BF16), preferred_element_type=F32)
        if oprev_ref is not None:
            oh = oh + oprev_ref[:, hs[h]]
        osum_ref[:, hs[h]] = oh
    tail = (1 + mxu_level) * C
    for h in range(heads):
        whole = ex_ref[tail:tail + 1, hs[h]]
        ke = (k[h] * jnp.exp(whole - ex_ref[0:C, hs[h]])).astype(BF16)
        decay = jnp.exp(whole)
        st_ref[h] = decay * st[h] + jnp.dot(v[h].T.astype(BF16), ke, preferred_element_type=F32)
    if readout:
        gate = gate_ref[...]
        o_ref[...] = (_rms(osum_ref[...], gain_ref[...]) * (gate * _sigmoid(gate))).astype(o_ref.dtype)


def gla_scan(proj, *, batch, seq, ctx_len, reverse, o_prev=None, readout_gain=None):
    _, M, D = proj.shape
    C = GLA_CHUNK
    heads = min(GLA_HEADS_PER_STEP, D // HEAD_DIM)
    width = heads * HEAD_DIM
    readout = readout_gain is not None
    assert not readout or width == D, "the fused readout normalises whole rows"
    n_ctx, n_lat = ctx_len // C, seq // C
    ctx_base = batch * seq // C
    mcat, lvl = _gla_tables(reverse)
    mcat = jnp.asarray(mcat, BF16)
    lvl = jnp.asarray(lvl)
    g_section = 3 if reverse else 2

    def row_block(b, n):
        if reverse:
            return jnp.where(n < n_ctx, ctx_base + b * n_ctx + (n_ctx - 1 - n),
                             b * n_lat + (n_lat - 1 - (n - n_ctx)))
        return jnp.where(n < n_ctx, ctx_base + b * n_ctx + n, b * n_lat + (n - n_ctx))

    def section(s):
        return pl.BlockSpec((None, C, width), lambda b, hg, n: (s, row_block(b, n), hg))

    row = pl.BlockSpec((C, width), lambda b, hg, n: (row_block(b, n), hg))
    const = lambda shape: pl.BlockSpec(shape, lambda b, hg, n: (0, 0))
    args = [proj, proj, proj, mcat, lvl]
    in_specs = [section(0), section(1), section(g_section), const(mcat.shape), const(lvl.shape)]
    scratch = [pltpu.VMEM((heads, HEAD_DIM, HEAD_DIM), F32),
               pltpu.VMEM(mcat.shape[:1] + (width,), F32)]
    if o_prev is not None:
        args.append(o_prev)
        in_specs.append(row)
    if readout:
        args += [proj, readout_gain.reshape(1, D)]
        in_specs += [section(4), const((1, D))]
        scratch.append(pltpu.VMEM((C, width), F32))
    body = functools.partial(_gla_body, heads=heads, reverse=reverse, readout=readout)
    if o_prev is None:
        body = functools.partial(_gla_no_prev, body)
    return pl.pallas_call(
        body,
        grid=(batch, D // width, n_ctx + n_lat),
        in_specs=in_specs,
        out_specs=row,
        out_shape=jax.ShapeDtypeStruct((M, D), BF16 if readout else F32),
        scratch_shapes=scratch,
        input_output_aliases={5: 0} if (o_prev is not None and not readout) else {},
        compiler_params=_params(("parallel", "parallel", "arbitrary")),
        name="gla_bwd" if reverse else "gla_fwd",
    )(*args)


def _gla_no_prev(body, q_ref, v_ref, g_ref, mcat_ref, lvl_ref, *rest):
    body(q_ref, v_ref, g_ref, mcat_ref, lvl_ref, None, *rest)


def _rope_epilogue(acc, o_ref, cos_ref, sin_ref, *, every):
    cos, sin = cos_ref[...], sin_ref[...]
    lane = lax.broadcasted_iota(jnp.int32, cos.shape, 1)
    first_half = (lane % 32) < 16
    for hh in range(acc.shape[1] // V7X_LANES):
        sl = slice(hh * V7X_LANES, (hh + 1) * V7X_LANES)
        x = acc[:, sl]
        if hh % every == every - 1:
            rot = jnp.where(first_half, -pltpu.roll(x, V7X_LANES - 16, 1), pltpu.roll(x, 16, 1))
            x = x * cos + rot * sin
        o_ref[:, sl] = x.astype(o_ref.dtype)


def rope_matmul(a, w, cos, sin, name, every=1):
    M = a.shape[0]
    tm = _tile(M, 1024, 8)
    tab = pl.BlockSpec((tm, V7X_LANES), lambda i, j, k: (i, 0))
    return matmul(a, w, out_dtype=BF16, epilogue=functools.partial(_rope_epilogue, every=every),
                  aux=((cos, tab), (sin, tab)), tm=tm, name=name)


def _attn_latent_body(q_ref, knc_ref, krc_ref, vc_ref, knl_ref, krl_ref, vl_ref, o_ref,
                      kcat_ref, vaug_ref, *, ctx_len, chunk):
    @pl.when(pl.program_id(2) == 0)
    def _():
        kcat_ref[0:ctx_len, 0:HEAD_DIM] = knc_ref[...]
        kcat_ref[0:ctx_len, HEAD_DIM:] = krc_ref[...]
        kcat_ref[ctx_len:, 0:HEAD_DIM] = knl_ref[...]
        kcat_ref[ctx_len:, HEAD_DIM:] = krl_ref[...]
        lane = lax.broadcasted_iota(jnp.int32, (vaug_ref.shape[0], HEAD_DIM), 1)
        vaug_ref[:, HEAD_DIM:] = jnp.where(lane == 0, 1.0, 0.0).astype(BF16)
        vaug_ref[0:ctx_len, 0:HEAD_DIM] = vc_ref[...]
        vaug_ref[ctx_len:, 0:HEAD_DIM] = vl_ref[...]

    q = q_ref[...]
    total = kcat_ref.shape[0]
    spans = [(0, ctx_len)] + [(c, chunk) for c in range(ctx_len, total, chunk)]
    m = acc = None
    for start, size in spans:
        s = lax.dot_general(q, kcat_ref[start:start + size, :], _NT, preferred_element_type=F32)
        m_c = jnp.max(s, axis=-1, keepdims=True)
        m_new = m_c if m is None else jnp.maximum(m, m_c)
        p = jnp.exp2(s - m_new).astype(BF16)
        pv = jnp.dot(p, vaug_ref[start:start + size, :], preferred_element_type=F32)
        acc = pv if m is None else jnp.exp2(m - m_new) * acc + pv
        m = m_new
    o_ref[...] = (acc[:, :HEAD_DIM] / acc[:, HEAD_DIM:HEAD_DIM + 1]).astype(o_ref.dtype)


def _attn_context_body(q_ref, kn_ref, kr_ref, v_ref, alias_ref, o_ref):
    del alias_ref
    q = q_ref[...]
    s = (lax.dot_general(q[:, :HEAD_DIM], kn_ref[...], _NT, preferred_element_type=F32)
         + lax.dot_general(q[:, HEAD_DIM:], kr_ref[...], _NT, preferred_element_type=F32))
    p = jnp.exp2(s - jnp.max(s, axis=-1, keepdims=True))
    o = jnp.dot(p.astype(BF16), v_ref[...], preferred_element_type=F32)
    o_ref[...] = (o / jnp.sum(p, axis=-1, keepdims=True)).astype(o_ref.dtype)


def mla_attention(q, kv, kr, *, batch, seq, ctx_len, context_queries=True):
    M = kv.shape[0]
    D = kv.shape[1] // 2
    H = D // HEAD_DIM
    tq = _tile(seq, ATTN_Q_TILE, 8)
    nq = seq // tq
    chunk = _tile(seq, ATTN_KEY_CHUNK, 8)
    ctx_base = batch * seq // ctx_len
    blk = lambda rows, fn: pl.BlockSpec((rows, HEAD_DIM), fn)
    ctx_keys = [blk(ctx_len, lambda b, h, i: (ctx_base + b, 2 * h)),
                blk(ctx_len, lambda b, h, i: (ctx_base + b, 0)),
                blk(ctx_len, lambda b, h, i: (ctx_base + b, 2 * h + 1))]
    lat_keys = [blk(seq, lambda b, h, i: (b, 2 * h)),
                blk(seq, lambda b, h, i: (b, 0)),
                blk(seq, lambda b, h, i: (b, 2 * h + 1))]
    q_lat = pl.BlockSpec((tq, 2 * HEAD_DIM), lambda b, h, i: (b * nq + i, h))
    o_lat_spec = blk(tq, lambda b, h, i: (b * nq + i, h))
    o_lat = pl.pallas_call(
        functools.partial(_attn_latent_body, ctx_len=ctx_len, chunk=chunk),
        grid=(batch, H, nq),
        in_specs=[q_lat] + ctx_keys + lat_keys,
        out_specs=o_lat_spec,
        out_shape=jax.ShapeDtypeStruct((M, D), BF16),
        scratch_shapes=[pltpu.VMEM((ctx_len + seq, 2 * HEAD_DIM), BF16),
                        pltpu.VMEM((ctx_len + seq, 2 * HEAD_DIM), BF16)],
        compiler_params=_params(("parallel", "parallel", "arbitrary")),
        name="mla_attention_latent",
    )(q, kv, kr, kv, kv, kr, kv)
    if not context_queries:
        return o_lat
    q_ctx = pl.BlockSpec((ctx_len, 2 * HEAD_DIM), lambda b, h, i: (ctx_base + b, h))
    o_ctx = blk(ctx_len, lambda b, h, i: (ctx_base + b, h))
    return pl.pallas_call(
        _attn_context_body,
        grid=(batch, H, 1),
        in_specs=[q_ctx] + ctx_keys + [pl.BlockSpec(memory_space=pl.ANY)],
        out_specs=o_ctx,
        out_shape=jax.ShapeDtypeStruct((M, D), BF16),
        input_output_aliases={4: 0},
        compiler_params=_params(("parallel", "parallel", "arbitrary")),
        name="mla_attention_context",
    )(q, kv, kr, kv, o_lat)


def _rope_tables(batch, seq, ctx_len):
    t = jnp.arange(seq)
    row = (t // GRID_W).astype(F32)
    col = (t % GRID_W).astype(F32)
    axis_dim = MLA_ROPE // 2
    inv_freq = ROPE_THETA ** (-jnp.arange(0, axis_dim, 2, dtype=F32) / axis_dim)
    ang_r = row[:, None] * inv_freq
    ang_c = col[:, None] * inv_freq
    ang = jnp.concatenate([ang_r, ang_r, ang_c, ang_c], axis=-1)
    pad = V7X_LANES - MLA_ROPE
    cos = jnp.pad(jnp.cos(ang), ((0, 0), (0, pad)), constant_values=1.0)
    sin = jnp.pad(jnp.sin(ang), ((0, 0), (0, pad)))
    n_ctx = batch * ctx_len
    cos = jnp.concatenate([jnp.tile(cos, (batch, 1)), jnp.ones((n_ctx, V7X_LANES), F32)], axis=0)
    sin = jnp.concatenate([jnp.tile(sin, (batch, 1)), jnp.zeros((n_ctx, V7X_LANES), F32)], axis=0)
    return cos, sin


def _pad_heads(w, heads, width):
    K = w.shape[0]
    w = w.reshape(K, heads, width)
    w = jnp.pad(w, ((0, 0), (0, 0), (0, V7X_LANES - width)))
    return w.reshape(K, heads * V7X_LANES).astype(BF16)


def kernel(x, c, ctx, c_ctx, ada_down, ada_up, ada_bias, norm_mix_pre, norm_mix_post, norm_mlp_pre,
           norm_mlp_post, mlp_w1, mlp_w2, hg_w_in, hg_lb_logits, hg_norm, hg_w_out, mla_w_down,
           mla_q_norm, mla_w_uq, mla_kv_norm, mla_w_ukv, mla_w_out):
    B, T, D = x.shape
    Lc = ctx.shape[1]
    depth = ada_down.shape[0]
    H = D // HEAD_DIM
    n_lat, n_ctx = B * T, B * Lc
    M = n_lat + n_ctx
    G = math.gcd(T, n_ctx)
    assert T % GLA_CHUNK == 0 and Lc % GLA_CHUNK == 0 and T % Lc == 0 and D % HEAD_DIM == 0

    cond = jnp.concatenate([c, c_ctx[None, :], jnp.zeros((7 - B % 8, D), F32)], axis=0)
    mods = ada_modulations(cond, ada_down, ada_up, ada_bias)
    group_src = np.concatenate([np.repeat(np.arange(B), T // G), np.full(n_ctx // G, B)])
    mods = mods[:, group_src, :].reshape(depth, M // G, 1, N_MOD, D)
    mod = lambda layer, idx: mods[layer, :, :, idx, :]

    lb = jnp.cumsum(jax.nn.softmax(hg_lb_logits.astype(F32), axis=0), axis=0)
    lb = lb - lb[:1]

    cos, sin = _rope_tables(B, T, Lc)
    hg_w_in_b, hg_w_out_b, mla_w_out_b = (w.astype(BF16) for w in (hg_w_in, hg_w_out, mla_w_out))
    mlp_w1_b, mlp_w2_b = mlp_w1.astype(BF16), mlp_w2.astype(BF16)
    xs, h = embed_norm_modulate(x.reshape(n_lat, D), ctx.reshape(n_ctx, D), norm_mix_pre[0],
                                mod(0, 0), mod(0, 1), G)

    for layer in range(depth):
        j = layer // 2
        rows = n_lat if layer == depth - 1 else M
        if layer % 2 == 0:
            proj = hgrn2_project(h, hg_w_in_b, j, lb[j])
            o = gla_scan(proj, batch=B, seq=T, ctx_len=Lc, reverse=False)
            r = gla_scan(proj, batch=B, seq=T, ctx_len=Lc, reverse=True, o_prev=o,
                         readout_gain=hg_norm[j])
            y = matmul(r, hg_w_out_b, out_dtype=F32, rows=rows, layer=j, name="hgrn2_out")
        else:
            w_down = mla_w_down[j]
            n_lat_rank = MLA_Q_RANK + MLA_KV_RANK
            lat = matmul(h, w_down[:, :n_lat_rank].astype(BF16), out_dtype=F32, tn=512,
                         name="mla_down")
            kr = rope_matmul(h, _pad_heads(w_down[:, n_lat_rank:], 1, MLA_ROPE), cos, sin,
                             "mla_rope_key")
            cq, ckv = mla_latent_norms(lat, mla_q_norm[j], mla_kv_norm[j])
            w_uq = (mla_w_uq[j] * ATTN_EXP2_SCALE).reshape(MLA_Q_RANK, H, HEAD_DIM + MLA_ROPE)
            w_q = jnp.pad(w_uq, ((0, 0), (0, 0), (0, 2 * HEAD_DIM - w_uq.shape[2])))
            q = rope_matmul(cq, w_q.reshape(MLA_Q_RANK, 2 * D).astype(BF16), cos, sin, "mla_q",
                            every=2)
            kv = matmul(ckv, mla_w_ukv[j].astype(BF16), out_dtype=BF16, name="mla_kv")
            o = mla_attention(q, kv, kr, batch=B, seq=T, ctx_len=Lc,
                              context_queries=rows == M)
            y = matmul(o, mla_w_out_b, out_dtype=F32, rows=rows, layer=j, name="mla_out")
        xs, h = residual_norm(xs, y, norm_mix_post[layer], mod(layer, 2), G,
                              nxt=(norm_mlp_pre[layer], mod(layer, 3), mod(layer, 4)))
        hid = matmul(h, mlp_w1_b, out_dtype=BF16, epilogue=_epi_relu2, layer=layer, name="mlp_up")
        m = matmul(hid, mlp_w2_b, out_dtype=F32, layer=layer, name="mlp_down")
        nxt = None
        if layer + 1 < depth:
            nxt = (norm_mix_pre[layer + 1], mod(layer + 1, 0), mod(layer + 1, 1))
        xs, h = residual_norm(xs, m, norm_mlp_post[layer], mod(layer, 5), G, nxt=nxt)
    return xs.reshape(B, T, D)
```
